```python
import jax, jax.numpy as jnp
from jax import lax
import numpy as np

D_MODEL = 1024
BATCH = 2
SEQ = 8192
DEPTH = 1
DEC_BATCH = 128
DEC_SEQ = 8
PAST_LEN = 8192
PAGE_SIZE = 128

N_HEADS_A = 16
HEAD_DIM = 64
N_KV = 4
C_A = N_HEADS_A * HEAD_DIM
KV_W = N_KV * HEAD_DIM
N_IDX_HEADS = 8
D_IDX = 64
TOPK_MAX = 256
Q_BLOCK = 128
ROPE_THETA = 10000.0
HEAD_B = 64
N_HEADS_B = D_MODEL // HEAD_B
C_B = N_HEADS_B * HEAD_B
R_W = 64
R_A = 64
N_SHIFT = 3 * C_B + R_W + R_A
GN_EPS = 64e-5
RMS_EPS = 1e-6
LN_EPS = 1e-6

SEGMENTS = (("q", C_A), ("k", KV_W), ("v", KV_W), ("iq", N_IDX_HEADS * D_IDX), ("ik", D_IDX),
            ("iw", N_IDX_HEADS), ("za", C_A), ("rw", N_SHIFT), ("zb", C_B), ("ga", D_MODEL), ("gb", D_MODEL))
N_IN = 2 * C_A + 2 * KV_W + N_IDX_HEADS * D_IDX + D_IDX + N_IDX_HEADS + N_SHIFT + C_B + 2 * D_MODEL

kernel_name = "hybrid_dsa_rwkv7_step"


def rms_norm(x, w):
    xf = x.astype(jnp.float32)
    y = xf * lax.rsqrt(jnp.mean(xf * xf, axis=-1, keepdims=True) + RMS_EPS)
    return (y * w.astype(jnp.float32)).astype(x.dtype)


def layer_norm(x, w, b):
    xf = x.astype(jnp.float32)
    mu = jnp.mean(xf, axis=-1, keepdims=True)
    var = jnp.mean(jnp.square(xf - mu), axis=-1, keepdims=True)
    y = (xf - mu) * lax.rsqrt(var + LN_EPS)
    return (y * w.astype(jnp.float32) + b.astype(jnp.float32)).astype(x.dtype)


def rope(x, pos):
    half = x.shape[-1] // 2
    inv = jnp.power(ROPE_THETA, -jnp.arange(half, dtype=jnp.float32) / half)
    ang = pos.astype(jnp.float32)[:, None] * inv[None, :]
    cos = jnp.cos(ang)[None, :, None, :]
    sin = jnp.sin(ang)[None, :, None, :]
    xf = x.astype(jnp.float32)
    x1, x2 = xf[..., :half], xf[..., half:]
    return jnp.concatenate([x1 * cos - x2 * sin, x2 * cos + x1 * sin], axis=-1).astype(x.dtype)


def project(h, pos, w_in, ik_ln_w, ik_ln_b):
    B, T, _ = h.shape
    p = jnp.einsum('btd,dn->btn', h, w_in)
    seg = {}
    off = 0
    for name, size in SEGMENTS:
        seg[name] = p[..., off:off + size]
        off += size
    q = rope(seg["q"].reshape(B, T, N_HEADS_A, HEAD_DIM), pos)
    k = rope(seg["k"].reshape(B, T, N_KV, HEAD_DIM), pos)
    v = seg["v"].reshape(B, T, N_KV, HEAD_DIM)
    iq = rope(seg["iq"].reshape(B, T, N_IDX_HEADS, D_IDX), pos)
    ik = rope(layer_norm(seg["ik"], ik_ln_w, ik_ln_b)[:, :, None, :], pos)[:, :, 0, :]
    iw = seg["iw"] * (N_IDX_HEADS ** -0.5 * D_IDX ** -0.5)
    return q, k, v, iq, iw, ik, seg["za"], seg["rw"], seg["zb"], seg["ga"], seg["gb"]


def indexer_scores(iq, iw, ik, q_pos, k_pos):
    dots = jnp.einsum('bqhd,bsd->bqhs', iq.astype(jnp.float32), ik.astype(jnp.float32))
    s = jnp.einsum('bqhs,bqh->bqs', jax.nn.relu(dots), iw.astype(jnp.float32))
    causal = k_pos[None, None, :] <= q_pos[None, :, None]
    return jnp.where(causal, s, -jnp.inf)


def sparse_attend(q, k_sel, v_sel, valid):
    B, Q = q.shape[:2]
    qg = q.reshape(B, Q, N_KV, N_HEADS_A // N_KV, HEAD_DIM).astype(jnp.float32)
    s = jnp.einsum('bqcgd,bqncd->bqcgn', qg, k_sel.astype(jnp.float32)) * (HEAD_DIM ** -0.5)
    s = jnp.where(valid[:, :, None, None, :], s, -jnp.inf)
    pr = jax.nn.softmax(s, axis=-1)
    o = jnp.einsum('bqcgn,bqncd->bqcgd', pr, v_sel.astype(jnp.float32))
    return o.reshape(B, Q, C_A).astype(q.dtype)


def gather_rows(rows, idx):
    return jax.vmap(lambda r, i: r[i])(rows, idx)


def attn_prompt(q, k, v, iq, iw, ik):
    B, S = q.shape[:2]
    topk = min(TOPK_MAX, S // 4)
    nblk = S // Q_BLOCK
    k_pos = jnp.arange(S)

    def block(args):
        qb, iqb, iwb, q_pos = args
        sc = indexer_scores(iqb, iwb, ik, q_pos, k_pos)
        _, sel = lax.top_k(sc, topk)
        valid = sel <= q_pos[None, :, None]
        return sparse_attend(qb, gather_rows(k, sel), gather_rows(v, sel), valid)

    to_blocks = lambda a: jnp.swapaxes(a.reshape((B, nblk, Q_BLOCK) + a.shape[2:]), 0, 1)
    out = lax.map(block, (to_blocks(q), to_blocks(iq), to_blocks(iw), jnp.arange(S).reshape(nblk, Q_BLOCK)))
    return jnp.swapaxes(out, 0, 1).reshape(B, S, C_A)


def attn_sample(q, k_new, v_new, iq, iw, ik_new, cache_k, cache_v, cache_ik, page_table):
    DB, T = q.shape[:2]
    page = cache_k.shape[1]
    past_len = page_table.shape[1] * page
    L = past_len + T
    topk = min(TOPK_MAX, L // 4)
    ik_past = cache_ik[page_table].reshape(DB, past_len, D_IDX)
    ik_all = jnp.concatenate([ik_past.astype(ik_new.dtype), ik_new], axis=1)
    q_pos = past_len + jnp.arange(T)
    sc = indexer_scores(iq, iw, ik_all, q_pos, jnp.arange(L))
    _, sel = lax.top_k(sc, topk)
    sel_past = jnp.minimum(sel, past_len - 1)
    phys = jnp.take_along_axis(page_table, (sel_past // page).reshape(DB, -1), axis=1).reshape(sel.shape)
    slot = sel_past % page
    sel_new = jnp.clip(sel - past_len, 0, T - 1)
    is_new = (sel >= past_len)[..., None, None]
    k_sel = jnp.where(is_new, gather_rows(k_new, sel_new), cache_k[phys, slot].astype(k_new.dtype))
    v_sel = jnp.where(is_new, gather_rows(v_new, sel_new), cache_v[phys, slot].astype(v_new.dtype))
    valid = sel <= q_pos[None, :, None]
    return sparse_attend(q, k_sel, v_sel, valid)


def rwkv_branch(p_rw, shift_prev, wkv0, mu, w0, w_up, a0, a_up, k_k, k_a, r_k, lnx_w, lnx_b):
    B, T, _ = p_rw.shape
    f32 = jnp.float32
    prev = jnp.concatenate([shift_prev[:, None, :].astype(p_rw.dtype), p_rw[:, :-1]], axis=1)
    xs = p_rw + (prev - p_rw) * mu
    r = xs[..., :C_B]
    k = xs[..., C_B:2 * C_B]
    v = xs[..., 2 * C_B:3 * C_B]
    wl = xs[..., 3 * C_B:3 * C_B + R_W]
    al = xs[..., 3 * C_B + R_W:]
    heads = lambda z: z.astype(f32).reshape(B, T, N_HEADS_B, HEAD_B)
    log_w = -jax.nn.softplus(-(w0.astype(f32) + jnp.tanh(wl.astype(f32)) @ w_up.astype(f32))) - 0.5
    a = jax.nn.sigmoid(a0.astype(f32) + al.astype(f32) @ a_up.astype(f32))
    kk = heads(k.astype(f32) * k_k.astype(f32))
    kk = kk / jnp.maximum(jnp.sqrt(jnp.sum(kk * kk, axis=-1, keepdims=True)), 1e-12)
    k2 = k.astype(f32) * (1.0 + (a - 1.0) * k_a.astype(f32))
    rh, kh, vh, ah = heads(r), heads(k2), heads(v), heads(a)
    decay = jnp.exp(-jnp.exp(heads(log_w)))
    bh = kk * ah

    def step(S, inp):
        r_t, d_t, k_t, v_t, kk_t, b_t = inp
        sa = jnp.einsum('bhvk,bhk->bhv', S, -kk_t)
        S = S * d_t[:, :, None, :] + sa[..., None] * b_t[:, :, None, :] + v_t[..., None] * k_t[:, :, None, :]
        return S, jnp.einsum('bhvk,bhk->bhv', S, r_t)

    tm = lambda z: jnp.swapaxes(z, 0, 1)
    S_T, ys = lax.scan(step, wkv0.astype(f32), (tm(rh), tm(decay), tm(kh), tm(vh), tm(kk), tm(bh)))
    y = jnp.swapaxes(ys, 0, 1)
    mu_y = jnp.mean(y, axis=-1, keepdims=True)
    var_y = jnp.mean(jnp.square(y - mu_y), axis=-1, keepdims=True)
    gn = ((y - mu_y) * lax.rsqrt(var_y + GN_EPS)).reshape(B, T, C_B) * lnx_w.astype(f32) + lnx_b.astype(f32)
    bonus = (jnp.sum(rh * kh * r_k.astype(f32), axis=-1, keepdims=True) * vh).reshape(B, T, C_B)
    return (gn + bonus).astype(p_rw.dtype), S_T.astype(p_rw.dtype), p_rw[:, -1]


def hybrid_layer(x, pos, attn_fn, shift_prev, wkv0, norm_w, w_in, ik_ln_w, ik_ln_b, rw_params, w_o):
    h = rms_norm(x, norm_w)
    q, k, v, iq, iw, ik, za, rw, zb, ga, gb = project(h, pos, w_in, ik_ln_w, ik_ln_b)
    attn = attn_fn(q, k, v, iq, iw, ik)
    rwo, wkv_new, shift_new = rwkv_branch(rw, shift_prev, wkv0, *rw_params)
    merged = jax.nn.sigmoid(ga) * (attn * jax.nn.silu(za)) + jax.nn.sigmoid(gb) * (rwo * jax.nn.silu(zb))
    x = x + jnp.einsum('btc,cd->btd', merged, w_o)
    return x, k, v, ik, wkv_new, shift_new


def setup_inputs(seed: int = 0) -> dict:
    key = jax.random.key(seed)
    ks = jax.random.split(key, 28)
    f = jnp.float32
    n_pages = PAST_LEN // PAGE_SIZE
    n_used = DEC_BATCH * n_pages
    n_pool = n_used + max(1, n_used // 4)
    perm = jax.random.permutation(ks[0], n_pool)
    page_table = perm[:n_used].reshape(DEC_BATCH, n_pages).astype(jnp.int32)
    nrm = lambda k, shape, scale: jax.random.normal(k, shape, f) * scale
    return {
        "x_prompt": nrm(ks[1], (BATCH, SEQ, D_MODEL), 1.0),
        "x_sample": nrm(ks[2], (DEC_BATCH, DEC_SEQ, D_MODEL), 1.0),
        "cache_k": nrm(ks[3], (DEPTH, n_pool, PAGE_SIZE, N_KV, HEAD_DIM), 1.0),
        "cache_v": nrm(ks[4], (DEPTH, n_pool, PAGE_SIZE, N_KV, HEAD_DIM), 1.0),
        "cache_idx_k": nrm(ks[5], (DEPTH, n_pool, PAGE_SIZE, D_IDX), 1.0),
        "state_wkv": nrm(ks[6], (DEPTH, DEC_BATCH, N_HEADS_B, HEAD_B, HEAD_B), 0.5),
        "state_shift": nrm(ks[7], (DEPTH, DEC_BATCH, N_SHIFT), 1.0),
        "page_table": page_table,
        "norm_w": 1.0 + nrm(ks[8], (DEPTH, D_MODEL), 0.02),
        "w_in": nrm(ks[9], (DEPTH, D_MODEL, N_IN), D_MODEL ** -0.5),
        "idx_k_ln_w": 1.0 + nrm(ks[10], (DEPTH, D_IDX), 0.02),
        "idx_k_ln_b": nrm(ks[11], (DEPTH, D_IDX), 0.02),
        "mu_shift": jax.random.uniform(ks[12], (DEPTH, N_SHIFT), f),
        "w0": jax.random.uniform(ks[13], (DEPTH, C_B), f, -6.0, -1.0),
        "w_up": nrm(ks[14], (DEPTH, R_W, C_B), 0.1),
        "a0": nrm(ks[15], (DEPTH, C_B), 0.1),
        "a_up": nrm(ks[16], (DEPTH, R_A, C_B), R_A ** -0.5),
        "k_k": 0.85 + nrm(ks[17], (DEPTH, C_B), 0.02),
        "k_a": 1.0 + nrm(ks[18], (DEPTH, C_B), 0.02),
        "r_k": nrm(ks[19], (DEPTH, N_HEADS_B, HEAD_B), 0.1),
        "ln_x_w": 1.0 + nrm(ks[20], (DEPTH, C_B), 0.02),
        "ln_x_b": nrm(ks[21], (DEPTH, C_B), 0.02),
        "w_o": nrm(ks[22], (DEPTH, D_MODEL, D_MODEL), D_MODEL ** -0.5),
        "final_norm_w": 1.0 + nrm(ks[23], (D_MODEL,), 0.02),
    }


def reference(x_prompt, x_sample, cache_k, cache_v, cache_idx_k, state_wkv, state_shift, page_table,
              norm_w, w_in, idx_k_ln_w, idx_k_ln_b, mu_shift, w0, w_up, a0, a_up, k_k, k_a, r_k,
              ln_x_w, ln_x_b, w_o, final_norm_w):
    B, S, _ = x_prompt.shape
    DB, T, _ = x_sample.shape
    past_len = page_table.shape[1] * cache_k.shape[2]
    pos_p = jnp.arange(S)
    pos_s = past_len + jnp.arange(T)
    xp, xs = x_prompt, x_sample
    kp, vp, ikp, wkvp, shp = [], [], [], [], []
    kss, vss, iks, wkvs, shs = [], [], [], [], []
    for l in range(DEPTH):
        rw_params = (mu_shift[l], w0[l], w_up[l], a0[l], a_up[l], k_k[l], k_a[l], r_k[l], ln_x_w[l], ln_x_b[l])
        xp, k_, v_, ik_, wkv_, sh_ = hybrid_layer(
            xp, pos_p, attn_prompt, jnp.zeros((B, N_SHIFT), xp.dtype),
            jnp.zeros((B, N_HEADS_B, HEAD_B, HEAD_B), jnp.float32),
            norm_w[l], w_in[l], idx_k_ln_w[l], idx_k_ln_b[l], rw_params, w_o[l])
        kp.append(k_); vp.append(v_); ikp.append(ik_); wkvp.append(wkv_); shp.append(sh_)
        ck, cv, cik = cache_k[l], cache_v[l], cache_idx_k[l]
        attn_s = lambda q, k, v, iq, iw, ik, ck=ck, cv=cv, cik=cik: attn_sample(q, k, v, iq, iw, ik, ck, cv, cik, page_table)
        xs, k_, v_, ik_, wkv_, sh_ = hybrid_layer(
            xs, pos_s, attn_s, state_shift[l], state_wkv[l],
            norm_w[l], w_in[l], idx_k_ln_w[l], idx_k_ln_b[l], rw_params, w_o[l])
        kss.append(k_); vss.append(v_); iks.append(ik_); wkvs.append(wkv_); shs.append(sh_)
    y_prompt = rms_norm(xp, final_norm_w)
    y_sample = rms_norm(xs, final_norm_w)
    return (y_prompt, y_sample,
            jnp.stack(kp), jnp.stack(vp), jnp.stack(ikp), jnp.stack(wkvp), jnp.stack(shp),
            jnp.stack(kss), jnp.stack(vss), jnp.stack(iks), jnp.stack(wkvs), jnp.stack(shs))
```

```python
import functools

import jax
import jax.numpy as jnp
from jax import lax
from jax.experimental import pallas as pl
from jax.experimental.pallas import tpu as pltpu

F32, BF16, I32 = jnp.float32, jnp.bfloat16, jnp.int32

D_MODEL = 1024
N_HEADS_A = 16
HEAD_DIM = 64
N_KV = 4
HEADS_PER_KV = N_HEADS_A // N_KV
C_A = N_HEADS_A * HEAD_DIM
KV_W = N_KV * HEAD_DIM
N_IDX_HEADS = 8
D_IDX = 64
IQ_W = N_IDX_HEADS * D_IDX
TOPK_MAX = 256
ROPE_THETA = 10000.0
HEAD_B = 64
N_HEADS_B = D_MODEL // HEAD_B
C_B = N_HEADS_B * HEAD_B
R_W = 64
R_A = 64
N_SHIFT = 3 * C_B + R_W + R_A
GN_EPS = 64e-5
RMS_EPS = 1e-6
LN_EPS = 1e-6
IW_SCALE = N_IDX_HEADS ** -0.5 * D_IDX ** -0.5
Q_SCALE = HEAD_DIM ** -0.5

_SEG_SIZES = (("q", C_A), ("k", KV_W), ("v", KV_W), ("iq", IQ_W), ("ik", D_IDX), ("iw", N_IDX_HEADS),
              ("za", C_A), ("rw", N_SHIFT), ("zb", C_B), ("ga", D_MODEL), ("gb", D_MODEL))

LANES = 128
SUBLANES = 8
HALF = HEAD_DIM // 2

INT_MIN = -(2 ** 31)
KEY_NEG_INF = 0x807FFFFF - 2 ** 32
M_INIT = -1e30


def _cp(sem, vmem_mb=None):
    kw = dict(dimension_semantics=sem)
    if vmem_mb is not None:
        kw["vmem_limit_bytes"] = vmem_mb << 20
    return pltpu.CompilerParams(**kw)


def _dot(a, b):
    return jnp.dot(a, b, preferred_element_type=F32)


def _dot_nt(a, b):
    return lax.dot_general(a, b, (((1,), (1,)), ((), ())), preferred_element_type=F32)


def _dot_tn(a, b):
    return lax.dot_general(a, b, (((0,), (0,)), ((), ())), preferred_element_type=F32)


def _split(x, parts):
    out = []
    for _ in range(parts):
        hi = x.astype(BF16)
        out.append(hi)
        x = x - hi.astype(F32)
    return out


def _dot_exact_rhs(a, b_bf16, parts):
    acc = None
    for t in _split(a, parts):
        d = _dot(t, b_bf16)
        acc = d if acc is None else acc + d
    return acc


def _dot_exact_lhs(a_bf16, b, parts):
    acc = None
    for t in _split(b, parts):
        d = _dot(a_bf16, t)
        acc = d if acc is None else acc + d
    return acc


def _mm3(a, b):
    a_hi, a_lo = _split(a, 2)
    b_hi, b_lo = _split(b, 2)
    return _dot(a_hi, b_hi) + (_dot(a_hi, b_lo) + _dot(a_lo, b_hi))


def _iota(shape, dim):
    return lax.broadcasted_iota(I32, shape, dim)


def _head_ones():
    r = _iota((LANES, LANES), 0) // HEAD_B
    c = _iota((LANES, LANES), 1) // HEAD_B
    return jnp.where(r == c, 1.0, 0.0).astype(BF16)


def _head_sum(x, g):
    return _dot_exact_rhs(x, g, 2)


def _rmsnorm_body(x_ref, w_ref, o_ref):
    x = x_ref[...]
    inv = lax.rsqrt(jnp.mean(x * x, axis=-1, keepdims=True) + RMS_EPS)
    o_ref[...] = (x * inv * w_ref[...]).astype(o_ref.dtype)


def _rmsnorm(x2d, w, out_dtype, tm):
    rows, d = x2d.shape
    return pl.pallas_call(
        _rmsnorm_body,
        out_shape=jax.ShapeDtypeStruct((rows, d), out_dtype),
        grid=(rows // tm,),
        in_specs=[pl.BlockSpec((tm, d), lambda i: (i, 0)), pl.BlockSpec((1, d), lambda i: (0, 0))],
        out_specs=pl.BlockSpec((tm, d), lambda i: (i, 0)),
        compiler_params=_cp(("parallel",)),
        name="rmsnorm",
    )(x2d, w.reshape(1, d))


def _proj_plain_body(h_ref, w_ref, o_ref):
    o_ref[...] = _dot(h_ref[...], w_ref[...]).astype(o_ref.dtype)


def _proj_plain(h, w, tm, tn, name):
    rows, d = h.shape
    n = w.shape[1]
    return pl.pallas_call(
        _proj_plain_body,
        out_shape=jax.ShapeDtypeStruct((rows, n), F32),
        grid=(rows // tm, n // tn),
        in_specs=[pl.BlockSpec((tm, d), lambda i, j: (i, 0)), pl.BlockSpec((d, tn), lambda i, j: (0, j))],
        out_specs=pl.BlockSpec((tm, tn), lambda i, j: (i, j)),
        compiler_params=_cp(("parallel", "arbitrary")),
        name=name,
    )(h, w)


def _rope_rows(x, cos_t, sin_t):
    first = (_iota((x.shape[0], LANES), 1) % HEAD_DIM) < HALF
    outs = []
    for c in range(x.shape[1] // LANES):
        xc = x[:, c * LANES:(c + 1) * LANES]
        partner = jnp.where(first, pltpu.roll(xc, LANES - HALF, 1), pltpu.roll(xc, HALF, 1))
        outs.append(xc * cos_t + partner * sin_t)
    return outs[0] if len(outs) == 1 else jnp.concatenate(outs, axis=1)


def _proj_rope_body(h_ref, w_ref, cos_ref, sin_ref, o_ref, obf_ref):
    y = _rope_rows(_dot(h_ref[...], w_ref[...]), cos_ref[...], sin_ref[...])
    o_ref[...] = y
    obf_ref[...] = y.astype(BF16)


def _proj_rope(h, w, cos_t, sin_t, tm, tn, name):
    rows, d = h.shape
    n = w.shape[1]
    return pl.pallas_call(
        _proj_rope_body,
        out_shape=(jax.ShapeDtypeStruct((rows, n), F32), jax.ShapeDtypeStruct((rows, n), BF16)),
        grid=(rows // tm, n // tn),
        in_specs=[pl.BlockSpec((tm, d), lambda i, j: (i, 0)), pl.BlockSpec((d, tn), lambda i, j: (0, j)),
                  pl.BlockSpec((tm, LANES), lambda i, j: (i, 0)), pl.BlockSpec((tm, LANES), lambda i, j: (i, 0))],
        out_specs=(pl.BlockSpec((tm, tn), lambda i, j: (i, j)), pl.BlockSpec((tm, tn), lambda i, j: (i, j))),
        compiler_params=_cp(("parallel", "arbitrary")),
        name=name,
    )(h, w, cos_t, sin_t)


def _proj_ikw_body(h_ref, w_ref, cos_ref, sin_ref, lnw_ref, lnb_ref, o_ref):
    acc = _dot(h_ref[...], w_ref[...])
    lane = _iota(acc.shape, 1)
    is_ik = lane < D_IDX
    mu = jnp.sum(jnp.where(is_ik, acc, 0.0), axis=-1, keepdims=True) * (1.0 / D_IDX)
    dev = jnp.where(is_ik, acc - mu, 0.0)
    var = jnp.sum(dev * dev, axis=-1, keepdims=True) * (1.0 / D_IDX)
    y = dev * lax.rsqrt(var + LN_EPS) * lnw_ref[...] + lnb_ref[...]
    partner = jnp.where(lane < HALF, pltpu.roll(y, LANES - HALF, 1), pltpu.roll(y, HALF, 1))
    yr = y * cos_ref[...] + partner * sin_ref[...]
    o_ref[...] = jnp.where(is_ik, yr, jnp.where(lane < D_IDX + N_IDX_HEADS, acc * IW_SCALE, 0.0))


def _proj_ikw(h, w, cos_t, sin_t, lnw, lnb, tm):
    rows, d = h.shape
    row = lambda i: (i, 0)
    fixed = lambda i: (0, 0)
    return pl.pallas_call(
        _proj_ikw_body,
        out_shape=jax.ShapeDtypeStruct((rows, LANES), F32),
        grid=(rows // tm,),
        in_specs=[pl.BlockSpec((tm, d), row), pl.BlockSpec((d, LANES), fixed),
                  pl.BlockSpec((tm, LANES), row), pl.BlockSpec((tm, LANES), row),
                  pl.BlockSpec((1, LANES), fixed), pl.BlockSpec((1, LANES), fixed)],
        out_specs=pl.BlockSpec((tm, LANES), row),
        compiler_params=_cp(("parallel",)),
        name="proj_ikw",
    )(h, w, cos_t, sin_t, lnw, lnb)


def _proj_gate_body(h_ref, wz_ref, wg_ref, o_ref):
    h = h_ref[...]
    z = _dot(h, wz_ref[...])
    g = _dot(h, wg_ref[...])
    o_ref[...] = jax.nn.sigmoid(g) * (z * jax.nn.sigmoid(z))


def _proj_gate(h, wz, wg, tm, tn, name):
    rows, d = h.shape
    n = wz.shape[1]
    return pl.pallas_call(
        _proj_gate_body,
        out_shape=jax.ShapeDtypeStruct((rows, n), F32),
        grid=(rows // tm, n // tn),
        in_specs=[pl.BlockSpec((tm, d), lambda i, j: (i, 0)), pl.BlockSpec((d, tn), lambda i, j: (0, j)),
                  pl.BlockSpec((d, tn), lambda i, j: (0, j))],
        out_specs=pl.BlockSpec((tm, tn), lambda i, j: (i, j)),
        compiler_params=_cp(("parallel", "arbitrary")),
        name=name,
    )(h, wz, wg)


def _rope_cols(x, c, s):
    outs = []
    for hh in range(x.shape[0] // HEAD_DIM):
        x0 = x[hh * HEAD_DIM:hh * HEAD_DIM + HALF]
        x1 = x[hh * HEAD_DIM + HALF:(hh + 1) * HEAD_DIM]
        outs.append(x0 * c - x1 * s)
        outs.append(x1 * c + x0 * s)
    return jnp.concatenate(outs, axis=0)


def _proj_t_body(h_ref, wq_ref, wiq_ref, wv_ref, wiw_ref, cos_ref, sin_ref, oq_ref, oiq_ref, ov_ref, oiw_ref):
    h = h_ref[...]
    c = cos_ref[...]
    s = sin_ref[...]
    oq_ref[0] = (_rope_cols(_dot_nt(wq_ref[...], h), c, s) * Q_SCALE).astype(BF16)
    oiq_ref[0] = _rope_cols(_dot_nt(wiq_ref[...], h), c, s).astype(BF16)
    ov_ref[0] = _dot_nt(wv_ref[...], h).astype(BF16)
    oiw_ref[0] = _dot_nt(wiw_ref[...], h) * IW_SCALE


def _proj_t(h, wq_t, wiq_t, wv_t, wiw_t, cos_c, sin_c, batch, seq, tm):
    d = h.shape[1]
    nblk = seq // tm
    fixed = lambda b, i: (0, 0)
    col = lambda b, i: (b, 0, i)
    return pl.pallas_call(
        _proj_t_body,
        out_shape=(jax.ShapeDtypeStruct((batch, C_A, seq), BF16), jax.ShapeDtypeStruct((batch, IQ_W, seq), BF16),
                   jax.ShapeDtypeStruct((batch, KV_W, seq), BF16), jax.ShapeDtypeStruct((batch, N_IDX_HEADS, seq), F32)),
        grid=(batch, nblk),
        in_specs=[pl.BlockSpec((tm, d), lambda b, i: (b * nblk + i, 0)),
                  pl.BlockSpec((C_A, d), fixed), pl.BlockSpec((IQ_W, d), fixed), pl.BlockSpec((KV_W, d), fixed),
                  pl.BlockSpec((N_IDX_HEADS, d), fixed),
                  pl.BlockSpec((HALF, tm), lambda b, i: (0, i)), pl.BlockSpec((HALF, tm), lambda b, i: (0, i))],
        out_specs=(pl.BlockSpec((1, C_A, tm), col), pl.BlockSpec((1, IQ_W, tm), col),
                   pl.BlockSpec((1, KV_W, tm), col), pl.BlockSpec((1, N_IDX_HEADS, tm), col)),
        compiler_params=_cp(("parallel", "parallel"), 48),
        name="proj_transposed",
    )(h, wq_t, wiq_t, wv_t, wiw_t, cos_c, sin_c)


def _key_to_float(key):
    bits = key ^ ((key >> 31) & 0x7FFFFFFF)
    return lax.bitcast_convert_type(bits, F32)


def _kth_largest_key(count_ge, like, k, bits_per_round):
    zero = jnp.zeros_like(like)
    t = jnp.where(count_ge(_key_to_float(zero)) >= k, zero, jnp.full_like(like, INT_MIN))
    n_rounds = 31 // bits_per_round
    assert n_rounds * bits_per_round == 31 or bits_per_round == 1

    def round_fn(i, t):
        shift = 31 - bits_per_round * (i + 1)
        accepted = jnp.zeros_like(t)
        for v in range(1, 2 ** bits_per_round):
            cand = t | (jnp.int32(v) << shift)
            accepted = accepted + (count_ge(_key_to_float(cand)) >= k).astype(I32)
        return t | (accepted << shift)

    return lax.fori_loop(0, n_rounds, round_fn, t)


def _attn_prompt_body(k_ref, vt_ref, ik_ref, qt_ref, iqt_ref, iwt_ref, o_ref,
                      sc_scr, qbd_scr, iqc_scr, m_scr, l_scr, acc_scr, *, tq, tk, seq, topk):
    j = pl.program_id(1)
    nk = ((j + 1) * tq + tk - 1) // tk
    neg_inf = -jnp.inf

    for h in range(N_IDX_HEADS):
        iqc_scr[:, h * tq:(h + 1) * tq] = iqt_ref[0, h * D_IDX:(h + 1) * D_IDX, :]
    qbd_scr[...] = jnp.zeros(qbd_scr.shape, BF16)
    for h in range(N_HEADS_A):
        gp, hh = divmod(h, 2 * HEADS_PER_KV)
        gl = hh // HEADS_PER_KV
        qbd_scr[gp, gl * HEAD_DIM:(gl + 1) * HEAD_DIM, hh * tq:(hh + 1) * tq] = qt_ref[0, h * HEAD_DIM:(h + 1) * HEAD_DIM, :]
    iw = iwt_ref[0]
    t_idx = j * tq + _iota((tk, tq), 1)
    row = _iota((tk, tq), 0)

    def idx_chunk(c, carry):
        off = pl.multiple_of(c * tk, tk)
        dots = _dot(ik_ref[0, pl.ds(off, tk), :], iqc_scr[...])
        acc = jnp.zeros((tk, tq), F32)
        for h in range(N_IDX_HEADS):
            acc = acc + jnp.maximum(dots[:, h * tq:(h + 1) * tq], 0.0) * iw[h:h + 1, :]
        sc_scr[pl.ds(off, tk), :] = jnp.where(off + row <= t_idx, acc, neg_inf)
        return carry

    lax.fori_loop(0, nk, idx_chunk, 0)

    def count(pred):
        def body(c, acc):
            off = pl.multiple_of(c * tk, tk)
            m = pred(sc_scr[pl.ds(off, tk), :], off + row)
            return acc + jnp.sum(m.astype(I32), axis=0, keepdims=True)
        return lax.fori_loop(0, nk, body, jnp.zeros((1, tq), I32))

    t_key = _kth_largest_key(lambda cf: count(lambda v, s: v >= cf), jnp.zeros((1, tq), I32), topk, 1)
    no_thr = t_key <= KEY_NEG_INF
    t_f = jnp.where(no_thr, neg_inf, _key_to_float(t_key))
    n_gt = count(lambda v, s: v > t_f)
    n_ge = count(lambda v, s: v >= t_f)
    need = (n_ge > topk) & jnp.logical_not(no_thr)
    room = topk - n_gt
    nbits = (seq - 1).bit_length()

    def x_step(i, x):
        cand = x | (jnp.int32(1) << (nbits - 1 - i))
        before = count(lambda v, s: (v == t_f) & (s < cand))
        return jnp.where(before < room, cand, x)

    any_need = jnp.max(need.astype(I32))
    x_tie = lax.fori_loop(0, nbits * any_need, x_step, jnp.zeros((1, tq), I32))
    x_lim = jnp.where(no_thr, -1, jnp.where(need, x_tie, seq))

    m_scr[...] = jnp.full(m_scr.shape, M_INIT, F32)
    l_scr[...] = jnp.zeros(l_scr.shape, F32)
    acc_scr[...] = jnp.zeros(acc_scr.shape, F32)

    def att_chunk(c, carry):
        off = pl.multiple_of(c * tk, tk)
        v = sc_scr[pl.ds(off, tk), :]
        sel = (v > t_f) | ((v == t_f) & (off + row <= x_lim))
        bias = jnp.where(sel, 0.0, neg_inf)
        for gp in range(N_KV // 2):
            s8 = _dot(k_ref[0, pl.ds(off, tk), gp * LANES:(gp + 1) * LANES], qbd_scr[gp])
            for gl in range(2):
                g = 2 * gp + gl
                ps = []
                for hl in range(HEADS_PER_KV):
                    hh = gl * HEADS_PER_KV + hl
                    h = g * HEADS_PER_KV + hl
                    sh = s8[:, hh * tq:(hh + 1) * tq] + bias
                    m_old = m_scr[h:h + 1, :]
                    m_new = jnp.maximum(m_old, jnp.max(sh, axis=0, keepdims=True))
                    p = jnp.exp(sh - m_new)
                    alpha = jnp.exp(m_old - m_new)
                    l_scr[h:h + 1, :] = alpha * l_scr[h:h + 1, :] + jnp.sum(p, axis=0, keepdims=True)
                    m_scr[h:h + 1, :] = m_new
                    acc_scr[h * HEAD_DIM:(h + 1) * HEAD_DIM, :] = acc_scr[h * HEAD_DIM:(h + 1) * HEAD_DIM, :] * alpha
                    ps.append(p.astype(BF16))
                pv = _dot(vt_ref[0, g * HEAD_DIM:(g + 1) * HEAD_DIM, pl.ds(off, tk)], jnp.concatenate(ps, axis=1))
                for hl in range(HEADS_PER_KV):
                    h = g * HEADS_PER_KV + hl
                    acc_scr[h * HEAD_DIM:(h + 1) * HEAD_DIM, :] += pv[:, hl * tq:(hl + 1) * tq]
        return carry

    lax.fori_loop(0, nk, att_chunk, 0)

    for h in range(N_HEADS_A):
        acc_scr[h * HEAD_DIM:(h + 1) * HEAD_DIM, :] = acc_scr[h * HEAD_DIM:(h + 1) * HEAD_DIM, :] / l_scr[h:h + 1, :]
    o_ref[0] = acc_scr[...].T


def _attn_prompt(k_bf, vt, ik_bf, qt, iqt, iwt, topk, tq, tk):
    batch, seq, _ = k_bf.shape
    full = lambda b, j: (b, 0, 0)
    col = lambda b, j: (b, 0, j)
    body = functools.partial(_attn_prompt_body, tq=tq, tk=tk, seq=seq, topk=topk)
    return pl.pallas_call(
        body,
        out_shape=jax.ShapeDtypeStruct((batch, seq, C_A), F32),
        grid=(batch, seq // tq),
        in_specs=[pl.BlockSpec((1, seq, KV_W), full), pl.BlockSpec((1, KV_W, seq), full),
                  pl.BlockSpec((1, seq, D_IDX), full),
                  pl.BlockSpec((1, C_A, tq), col), pl.BlockSpec((1, IQ_W, tq), col),
                  pl.BlockSpec((1, N_IDX_HEADS, tq), col)],
        out_specs=pl.BlockSpec((1, tq, C_A), lambda b, j: (b, j, 0)),
        scratch_shapes=[pltpu.VMEM((seq, tq), F32),
                        pltpu.VMEM((N_KV // 2, LANES, 2 * HEADS_PER_KV * tq), BF16),
                        pltpu.VMEM((D_IDX, N_IDX_HEADS * tq), BF16),
                        pltpu.VMEM((N_HEADS_A, tq), F32), pltpu.VMEM((N_HEADS_A, tq), F32),
                        pltpu.VMEM((C_A, tq), F32)],
        compiler_params=_cp(("parallel", "arbitrary"), 56),
        name="attn_prompt",
    )(k_bf, vt, ik_bf, qt, iqt, iwt)


def _attn_sample_body(pt_ref, q2_ref, iq2_ref, iwc_ref, kn_ref, vn_ref, ikn_ref, ck_hbm, cv_hbm, cik_hbm, o_ref,
                      kbuf, vbuf, ikbuf, isc_scr, s_scr, sem, *, n_seq, n_pages, page, n_new, topk, tl):
    b = pl.program_id(0)
    slot = b % 2
    past = n_pages * page
    n_chunks = past // tl
    rows_q = N_HEADS_A * n_new
    rows_i = N_IDX_HEADS * n_new
    neg_inf = -jnp.inf

    def page_copies(seq_i, slot_i, p):
        pg = pt_ref[seq_i, p]
        dst = pl.ds(p * page, page)
        return (pltpu.make_async_copy(ck_hbm.at[pg], kbuf.at[slot_i, dst, :], sem.at[slot_i, 0]),
                pltpu.make_async_copy(cv_hbm.at[pg], vbuf.at[slot_i, dst, :], sem.at[slot_i, 1]),
                pltpu.make_async_copy(cik_hbm.at[pg], ikbuf.at[slot_i, dst, :], sem.at[slot_i, 2]))

    def start_seq(seq_i, slot_i):
        def body(p, carry):
            for cp in page_copies(seq_i, slot_i, p):
                cp.start()
            return carry
        lax.fori_loop(0, n_pages, body, 0)

    def wait_seq(seq_i, slot_i):
        def body(p, carry):
            for cp in page_copies(seq_i, slot_i, p):
                cp.wait()
            return carry
        lax.fori_loop(0, n_pages, body, 0)

    @pl.when(b == 0)
    def _():
        start_seq(0, 0)

    @pl.when(b + 1 < n_seq)
    def _():
        start_seq(b + 1, 1 - slot)

    wait_seq(b, slot)

    iq2 = iq2_ref[0]
    iwc = iwc_ref[0]

    def idx_scores(ik_rows):
        d = jnp.maximum(_dot_nt(iq2, ik_rows.astype(BF16)), 0.0) * iwc
        return jnp.sum(d.reshape(N_IDX_HEADS, n_new, d.shape[1]), axis=0)

    def idx_chunk(c, carry):
        off = pl.multiple_of(c * tl, tl)
        isc_scr[:, pl.ds(off, tl)] = idx_scores(ikbuf[slot, pl.ds(off, tl), :])
        return carry

    lax.fori_loop(0, n_chunks, idx_chunk, 0)
    pad_rows = LANES - n_new
    ik_new = jnp.concatenate([ikn_ref[0], jnp.zeros((pad_rows, D_IDX), F32)], axis=0)
    lane_n = _iota((n_new, LANES), 1)
    q_n = _iota((n_new, LANES), 0)
    isc_scr[:, past:past + LANES] = jnp.where(lane_n <= q_n, idx_scores(ik_new), neg_inf)
    total = past + LANES
    n_all = total // LANES
    lane_id = _iota((n_new, LANES), 1)

    def count(pred):
        def body(c, acc):
            off = pl.multiple_of(c * LANES, LANES)
            m = pred(isc_scr[:, pl.ds(off, LANES)], off + lane_id)
            return acc + jnp.where(m, 1.0, 0.0)
        acc = lax.fori_loop(0, n_all, body, jnp.zeros((n_new, LANES), F32))
        return jnp.sum(acc, axis=1, keepdims=True)

    kf = float(topk)
    t_key = _kth_largest_key(lambda cf: count(lambda v, s: v >= cf), jnp.zeros((n_new, 1), I32), kf, 1)
    no_thr = t_key <= KEY_NEG_INF
    t_f = jnp.where(no_thr, neg_inf, _key_to_float(t_key))
    n_gt = count(lambda v, s: v > t_f)
    n_ge = count(lambda v, s: v >= t_f)
    need = (n_ge > kf) & jnp.logical_not(no_thr)
    room = kf - n_gt
    nbits = (total - 1).bit_length()

    def x_step(i, x):
        cand = x | (jnp.int32(1) << (nbits - 1 - i))
        before = count(lambda v, s: (v == t_f) & (s < cand))
        return jnp.where(before < room, cand, x)

    any_need = jnp.max(need.astype(I32))
    x_tie = lax.fori_loop(0, nbits * any_need, x_step, jnp.zeros((n_new, 1), I32))
    x_lim = jnp.where(no_thr, -1, jnp.where(need, x_tie, total))

    def sel_bias(off, width):
        v = isc_scr[:, pl.ds(off, width)]
        s = off + _iota((n_new, width), 1)
        sel = (v > t_f) | ((v == t_f) & (s <= x_lim))
        b8 = jnp.where(sel, 0.0, neg_inf)
        return jnp.concatenate([b8] * N_HEADS_A, axis=0)

    q2 = q2_ref[0]

    def score_chunk(c, m):
        off = pl.multiple_of(c * tl, tl)
        s = _dot_nt(q2, kbuf[slot, pl.ds(off, tl), :].astype(BF16)) * Q_SCALE + sel_bias(off, tl)
        s_scr[:, pl.ds(off, tl)] = s
        return jnp.maximum(m, jnp.max(s, axis=1, keepdims=True))

    m = lax.fori_loop(0, n_chunks, score_chunk, jnp.full((rows_q, 1), M_INIT, F32))
    k_new = jnp.concatenate([kn_ref[0], jnp.zeros((pad_rows, KV_W), F32)], axis=0).astype(BF16)
    v_new = jnp.concatenate([vn_ref[0], jnp.zeros((pad_rows, KV_W), F32)], axis=0).astype(BF16)
    s_n = _dot_nt(q2, k_new) * Q_SCALE + sel_bias(past, LANES)
    m = jnp.maximum(m, jnp.max(s_n, axis=1, keepdims=True))
    p_n = jnp.exp(s_n - m)
    l0 = jnp.sum(p_n, axis=1, keepdims=True)
    o0 = _dot(p_n.astype(BF16), v_new)

    def pv_chunk(c, carry):
        l, o = carry
        off = pl.multiple_of(c * tl, tl)
        p = jnp.exp(s_scr[:, pl.ds(off, tl)] - m)
        l = l + jnp.sum(p, axis=1, keepdims=True)
        o = o + _dot(p.astype(BF16), vbuf[slot, pl.ds(off, tl), :].astype(BF16))
        return l, o

    l, o = lax.fori_loop(0, n_chunks, pv_chunk, (l0, o0))
    o_ref[0] = o / l


def _attn_sample(page_table, q2, iq2, iwc, k_new, v_new, ik_new, cache_k, cache_v, cache_ik, topk, tl):
    n_seq, n_pages = page_table.shape
    _, page, _ = cache_k.shape
    n_new = k_new.shape[1]
    past = n_pages * page
    rows_q = N_HEADS_A * n_new
    rows_i = N_IDX_HEADS * n_new
    per_seq = lambda b, pt: (b, 0, 0)
    body = functools.partial(_attn_sample_body, n_seq=n_seq, n_pages=n_pages, page=page, n_new=n_new, topk=topk, tl=tl)
    grid_spec = pltpu.PrefetchScalarGridSpec(
        num_scalar_prefetch=1,
        grid=(n_seq,),
        in_specs=[pl.BlockSpec((1, rows_q, KV_W), per_seq), pl.BlockSpec((1, rows_i, D_IDX), per_seq),
                  pl.BlockSpec((1, rows_i, 1), per_seq),
                  pl.BlockSpec((1, n_new, KV_W), per_seq), pl.BlockSpec((1, n_new, KV_W), per_seq),
                  pl.BlockSpec((1, n_new, D_IDX), per_seq),
                  pl.BlockSpec(memory_space=pl.ANY), pl.BlockSpec(memory_space=pl.ANY),
                  pl.BlockSpec(memory_space=pl.ANY)],
        out_specs=pl.BlockSpec((1, rows_q, KV_W), per_seq),
        scratch_shapes=[pltpu.VMEM((2, past, KV_W), F32), pltpu.VMEM((2, past, KV_W), F32),
                        pltpu.VMEM((2, past, D_IDX), F32),
                        pltpu.VMEM((n_new, past + LANES), F32), pltpu.VMEM((rows_q, past), F32),
                        pltpu.SemaphoreType.DMA((2, 3))],
    )
    return pl.pallas_call(
        body,
        out_shape=jax.ShapeDtypeStruct((n_seq, rows_q, KV_W), F32),
        grid_spec=grid_spec,
        compiler_params=_cp(("arbitrary",), 58),
        name="attn_sample",
    )(page_table, q2, iq2, iwc, k_new, v_new, ik_new, cache_k, cache_v, cache_ik)


def _rwkv_prep_body(p_ref, halo_ref, first_ref, mu_ref, w0_ref, wup_ref, a0_ref, aup_ref, kk_ref, ka_ref, rk_ref,
                    at_ref, rt_ref, kh_ref, bh_ref, kb_ref, bb_ref, v_ref, bonus_ref, pc_ref,
                    *, tb, chunk, blocks_per_seq):
    p = p_ref[...]
    row = _iota((tb, 1), 0)
    rolled = pltpu.roll(p, 1, 0)
    if blocks_per_seq is None:
        n_seq = tb // chunk
        first = first_ref[...]
        expanded = jnp.broadcast_to(first[:, None, :], (n_seq, chunk, N_SHIFT)).reshape(tb, N_SHIFT)
        prev = jnp.where(row % chunk == 0, expanded, rolled)
    else:
        i = pl.program_id(0)
        starts_seq = (i % blocks_per_seq) == 0
        row0 = jnp.where(starts_seq, first_ref[0], halo_ref[SUBLANES - 1:SUBLANES, :])
        prev = jnp.where(row == 0, row0, rolled)
    xs = p + (prev - p) * mu_ref[...]
    r = xs[:, 0:C_B]
    k = xs[:, C_B:2 * C_B]
    v = xs[:, 2 * C_B:3 * C_B]
    wa = xs[:, 3 * C_B:N_SHIFT]
    lane = _iota((tb, LANES), 1)
    t = jnp.where(lane < R_W, jnp.tanh(wa), wa).astype(BF16)
    log_w = -jax.nn.softplus(-(w0_ref[...] + _dot(t, wup_ref[...]))) - 0.5
    ld = -jnp.exp(log_w)
    a_sig = jax.nn.sigmoid(a0_ref[...] + _dot(t, aup_ref[...]))
    g = _head_ones()
    kk = k * kk_ref[...]
    k2 = k * (1.0 + (a_sig - 1.0) * ka_ref[...])
    rk = r * k2 * rk_ref[...]
    tri = jnp.where(_iota((chunk, chunk), 1) <= _iota((chunk, chunk), 0), 1.0, 0.0).astype(BF16)
    for ch in range(tb // chunk):
        rows = slice(ch * chunk, (ch + 1) * chunk)
        ld_c = ld[rows]
        cum = _dot_exact_lhs(tri, ld_c, 3)
        tot = cum[chunk - 1:chunk]
        rem = tot - cum
        pc_ref[0, ch:ch + 1, :] = jnp.exp(tot)
        e_cum = jnp.exp(cum)
        e_inv = jnp.exp(-cum)
        e_rem = jnp.exp(rem)
        e_prev = jnp.exp(cum - ld_c)
        for c in range(C_B // LANES):
            sl = slice(c * LANES, (c + 1) * LANES)
            kk_c = kk[rows, sl]
            nrm = jnp.maximum(jnp.sqrt(_head_sum(kk_c * kk_c, g)), 1e-12)
            kkn = kk_c / nrm
            bv = kkn * a_sig[rows, sl]
            at_ref[rows, sl] = (kkn * e_prev[:, sl]).astype(at_ref.dtype)
            rt_ref[rows, sl] = (r[rows, sl] * e_cum[:, sl]).astype(rt_ref.dtype)
            kh_ref[rows, sl] = (k2[rows, sl] * e_inv[:, sl]).astype(kh_ref.dtype)
            bh_ref[rows, sl] = (bv * e_inv[:, sl]).astype(bh_ref.dtype)
            kb_ref[rows, sl] = (k2[rows, sl] * e_rem[:, sl]).astype(kb_ref.dtype)
            bb_ref[rows, sl] = (bv * e_rem[:, sl]).astype(bb_ref.dtype)
            v_ref[rows, sl] = v[rows, sl].astype(v_ref.dtype)
            bonus_ref[rows, sl] = _head_sum(rk[rows, sl], g) * v[rows, sl]


def _rwkv_prep(p_rw, first_prev, params, tb, chunk, blocks_per_seq, store_dtype):
    rows = p_rw.shape[0]
    mu, w0, wup, a0, aup, kk, ka, rk = params
    nblk = rows // tb
    n_chunks = tb // chunk
    rowb = lambda i: (i, 0)
    fixed = lambda i: (0, 0)
    if blocks_per_seq is None:
        halo_spec = pl.BlockSpec((SUBLANES, N_SHIFT), fixed)
        first_spec = pl.BlockSpec((n_chunks, N_SHIFT), rowb)
    else:
        halo_spec = pl.BlockSpec((SUBLANES, N_SHIFT), lambda i: (jnp.maximum(i * (tb // SUBLANES) - 1, 0), 0))
        first_spec = pl.BlockSpec((1, 1, N_SHIFT), lambda i: (i // blocks_per_seq, 0, 0))
    vec = lambda n: pl.BlockSpec((1, n), fixed)
    wide = jax.ShapeDtypeStruct((rows, C_B), store_dtype)
    body = functools.partial(_rwkv_prep_body, tb=tb, chunk=chunk, blocks_per_seq=blocks_per_seq)
    return pl.pallas_call(
        body,
        out_shape=(wide,) * 7 + (jax.ShapeDtypeStruct((rows, C_B), F32),
                                 jax.ShapeDtypeStruct((nblk, n_chunks, C_B), F32)),
        grid=(nblk,),
        in_specs=[pl.BlockSpec((tb, N_SHIFT), rowb), halo_spec, first_spec, vec(N_SHIFT), vec(C_B),
                  pl.BlockSpec((LANES, C_B), fixed), vec(C_B), pl.BlockSpec((LANES, C_B), fixed),
                  vec(C_B), vec(C_B), vec(C_B)],
        out_specs=(pl.BlockSpec((tb, C_B), rowb),) * 8 + (pl.BlockSpec((1, n_chunks, C_B), lambda i: (i, 0, 0)),),
        compiler_params=_cp(("parallel",), 56),
        name="rwkv_prep",
    )(p_rw, p_rw, first_prev, mu, w0, wup, a0, aup, kk, ka, rk)


def _rwkv_scan_body(at_ref, rt_ref, kh_ref, bh_ref, kb_ref, bb_ref, v_ref, bonus_ref, pc_ref, s0_ref, lnw_ref, lnb_ref,
                    y_ref, sout_ref, s_scr, *, chunk, n_chunks):
    c = pl.program_id(1)
    two = 2 * chunk

    @pl.when(c == 0)
    def _():
        s_scr[...] = s0_ref[0]

    ri = _iota((two, two), 0)
    ci = _iota((two, two), 1)
    same = (ri // chunk) == (ci // chunk)
    strict = same & (ci < ri)
    incl = same & (ci <= ri)
    stack_mask = (_iota((two, LANES), 0) // chunk) == (_iota((two, LANES), 1) // HEAD_B)
    eye = jnp.where(ri == ci, 1.0, 0.0)
    g = _head_ones()
    n_factors = (chunk - 1).bit_length()

    def stack(ref, sl):
        x = ref[0, :, sl].astype(BF16)
        return jnp.where(stack_mask, jnp.concatenate([x, x], axis=0), jnp.zeros((two, LANES), BF16))

    for pr in range(N_HEADS_B // 2):
        sl = slice(pr * LANES, (pr + 1) * LANES)
        a_s, r_s, k_s, b_s = stack(at_ref, sl), stack(rt_ref, sl), stack(kh_ref, sl), stack(bh_ref, sl)
        kb_s, bb_s, v_s = stack(kb_ref, sl), stack(bb_ref, sl), stack(v_ref, sl)
        ar = jnp.concatenate([a_s, r_s], axis=0)
        m_ab = jnp.where(strict, _dot_nt(a_s, b_s), 0.0)
        m_ak = jnp.where(strict, _dot_nt(a_s, k_s), 0.0)
        m_rb = jnp.where(incl, _dot_nt(r_s, b_s), 0.0)
        m_rk = jnp.where(incl, _dot_nt(r_s, k_s), 0.0)
        pw = -m_ab
        t_inv = eye + pw
        for _ in range(n_factors - 1):
            pw = _mm3(pw, pw)
            t_inv = t_inv + _mm3(t_inv, pw)
        s_old = s_scr[pr]
        xs = _dot_nt(ar, s_old.astype(BF16))
        w = xs[0:two] + _dot(m_ak.astype(BF16), v_s)
        u_s = -_mm3(t_inv, w)
        u_bf = u_s.astype(BF16)
        y2 = xs[two:2 * two] + _dot(m_rk.astype(BF16), v_s) + _dot(m_rb.astype(BF16), u_bf)
        y = y2[0:chunk] + y2[chunk:two]
        s_scr[pr] = s_old * pc_ref[0, 0, :, sl] + _dot_tn(v_s, kb_s) + _dot_tn(u_bf, bb_s)
        mean = _head_sum(y, g) * (1.0 / HEAD_B)
        dev = y - mean
        var = _head_sum(dev * dev, g) * (1.0 / HEAD_B)
        y_ref[0, :, sl] = dev * lax.rsqrt(var + GN_EPS) * lnw_ref[:, sl] + lnb_ref[:, sl] + bonus_ref[0, :, sl]

    @pl.when(c == n_chunks - 1)
    def _():
        sout_ref[0] = s_scr[...]


def _rwkv_scan(prep, s0_bd, lnw, lnb, n_seq, n_chunks, chunk):
    at, rt, kh, bh, kb, bb, vv, bonus, pc = prep
    shp = lambda a: a.reshape(n_seq, n_chunks * chunk, C_B)
    pc4 = pc.reshape(n_seq, n_chunks, 1, C_B)
    tok = pl.BlockSpec((1, chunk, C_B), lambda b, c: (b, c, 0))
    state = pl.BlockSpec((1, N_HEADS_B // 2, LANES, LANES), lambda b, c: (b, 0, 0, 0))
    vec = pl.BlockSpec((1, C_B), lambda b, c: (0, 0))
    body = functools.partial(_rwkv_scan_body, chunk=chunk, n_chunks=n_chunks)
    return pl.pallas_call(
        body,
        out_shape=(jax.ShapeDtypeStruct((n_seq, n_chunks * chunk, C_B), F32),
                   jax.ShapeDtypeStruct((n_seq, N_HEADS_B // 2, LANES, LANES), F32)),
        grid=(n_seq, n_chunks),
        in_specs=[tok] * 8 + [pl.BlockSpec((1, 1, 1, C_B), lambda b, c: (b, c, 0, 0)), state, vec, vec],
        out_specs=(tok, state),
        scratch_shapes=[pltpu.VMEM((N_HEADS_B // 2, LANES, LANES), F32)],
        compiler_params=_cp(("parallel", "arbitrary"), 48),
        name="rwkv_scan",
    )(shp(at), shp(rt), shp(kh), shp(bh), shp(kb), shp(bb), shp(vv), shp(bonus), pc4, s0_bd, lnw, lnb)


def _merge_body(x_ref, attn_ref, rwo_ref, ga_ref, gb_ref, wo_ref, fw_ref, o_ref, *, final):
    merged = ga_ref[...] * attn_ref[...] + gb_ref[...] * rwo_ref[...]
    xn = x_ref[...] + _dot(merged.astype(BF16), wo_ref[...])
    if final:
        xn = xn * lax.rsqrt(jnp.mean(xn * xn, axis=-1, keepdims=True) + RMS_EPS) * fw_ref[...]
    o_ref[...] = xn


def _merge(x2d, attn, rwo, gate_a, gate_b, wo, fw, tm, final):
    rows, d = x2d.shape
    rowb = lambda i: (i, 0)
    fixed = lambda i: (0, 0)
    return pl.pallas_call(
        functools.partial(_merge_body, final=final),
        out_shape=jax.ShapeDtypeStruct((rows, d), F32),
        grid=(rows // tm,),
        in_specs=[pl.BlockSpec((tm, d), rowb)] * 5 + [pl.BlockSpec((d, d), fixed), pl.BlockSpec((1, d), fixed)],
        out_specs=pl.BlockSpec((tm, d), rowb),
        compiler_params=_cp(("parallel",), 48),
        name="merge_out",
    )(x2d, attn, rwo, gate_a, gate_b, wo, fw)


def _rope_tables(pos):
    inv = jnp.power(ROPE_THETA, -jnp.arange(HALF, dtype=F32) / HALF)
    ang = pos.astype(F32)[:, None] * inv[None, :]
    return jnp.cos(ang), jnp.sin(ang)


def _row_tables(cos, sin):
    return jnp.tile(cos, (1, LANES // HALF)), jnp.tile(jnp.concatenate([-sin, sin], axis=1), (1, LANES // HEAD_DIM))


def _pad_rows_to(x, rows, at):
    out = jnp.zeros((rows,) + x.shape[1:], x.dtype)
    return lax.dynamic_update_slice_in_dim(out, x, at, axis=0)


def _block_diag_state(s):
    n = s.shape[0]
    s = s.reshape(n, N_HEADS_B // 2, 2, HEAD_B, HEAD_B)
    z = jnp.zeros_like(s[:, :, 0])
    top = jnp.concatenate([s[:, :, 0], z], axis=-1)
    bot = jnp.concatenate([z, s[:, :, 1]], axis=-1)
    return jnp.concatenate([top, bot], axis=-2)


def _diag_state(s_bd):
    n = s_bd.shape[0]
    a = s_bd[:, :, :HEAD_B, :HEAD_B]
    b = s_bd[:, :, HEAD_B:, HEAD_B:]
    return jnp.stack([a, b], axis=2).reshape(n, N_HEADS_B, HEAD_B, HEAD_B)


def _layer_weights(w_in_l, idx_k_ln_w_l, idx_k_ln_b_l, w_up_l, a_up_l):
    seg, off = {}, 0
    for name, size in _SEG_SIZES:
        seg[name] = w_in_l[:, off:off + size].astype(BF16)
        off += size
    pad = jnp.zeros((D_MODEL, LANES - D_IDX - N_IDX_HEADS), BF16)
    seg["ikw"] = jnp.concatenate([seg["ik"], seg["iw"], pad], axis=1)
    zeros = jnp.zeros((LANES - D_IDX,), F32)
    seg["lnw"] = jnp.concatenate([idx_k_ln_w_l, zeros]).reshape(1, LANES)
    seg["lnb"] = jnp.concatenate([idx_k_ln_b_l, zeros]).reshape(1, LANES)
    seg["wup"] = _pad_rows_to(w_up_l.astype(BF16), LANES, 0)
    seg["aup"] = _pad_rows_to(a_up_l.astype(BF16), LANES, R_W)
    return seg


def _rwkv_branch(p_rw, first_prev, s0_bd, rw_params, lnw, lnb, n_seq, seq_len, chunk, tb, blocks_per_seq, store_dtype):
    prep = _rwkv_prep(p_rw, first_prev, rw_params, tb, chunk, blocks_per_seq, store_dtype)
    y, s_bd = _rwkv_scan(prep, s0_bd, lnw, lnb, n_seq, seq_len // chunk, chunk)
    return y.reshape(n_seq * seq_len, C_B), _diag_state(s_bd)


def kernel(x_prompt, x_sample, cache_k, cache_v, cache_idx_k, state_wkv, state_shift, page_table, norm_w, w_in, idx_k_ln_w, idx_k_ln_b, mu_shift, w0, w_up, a0, a_up, k_k, k_a, r_k, ln_x_w, ln_x_b, w_o, final_norm_w):
    batch, seq, _ = x_prompt.shape
    n_dec, n_new, _ = x_sample.shape
    depth = w_in.shape[0]
    n_pool, page = cache_k.shape[1], cache_k.shape[2]
    n_pages = page_table.shape[1]
    past = n_pages * page
    rows_p, rows_s = batch * seq, n_dec * n_new

    cos_p, sin_p = _rope_tables(jnp.arange(seq))
    cos_s, sin_s = _rope_tables(past + jnp.arange(n_new))
    cos_s, sin_s = jnp.tile(cos_s, (n_dec, 1)), jnp.tile(sin_s, (n_dec, 1))
    cos_pr, sin_pr = _row_tables(jnp.tile(cos_p, (batch, 1)), jnp.tile(sin_p, (batch, 1)))
    cos_sr, sin_sr = _row_tables(cos_s, sin_s)
    cos_pc, sin_pc = cos_p.T, sin_p.T

    xp = x_prompt.reshape(rows_p, D_MODEL)
    xs = x_sample.reshape(rows_s, D_MODEL)
    outs_p = {n: [] for n in ("k", "v", "ik", "wkv", "sh")}
    outs_s = {n: [] for n in ("k", "v", "ik", "wkv", "sh")}
    head_group = jnp.arange(N_HEADS_A) // HEADS_PER_KV
    kv_onehot = head_group[:, None] == jnp.arange(N_KV)[None, :]

    for l in range(depth):
        wl = _layer_weights(w_in[l], idx_k_ln_w[l], idx_k_ln_b[l], w_up[l], a_up[l])
        vec = lambda a: a.reshape(1, -1)
        rw_params = (vec(mu_shift[l]), vec(w0[l]), wl["wup"], vec(a0[l]), wl["aup"], vec(k_k[l]), vec(k_a[l]), vec(r_k[l]))
        lnw, lnb = vec(ln_x_w[l]), vec(ln_x_b[l])
        wo = w_o[l].astype(BF16)
        final = l == depth - 1
        fw = vec(final_norm_w)

        h = _rmsnorm(xp, norm_w[l], BF16, 512)
        k_p, k_bf = _proj_rope(h, wl["k"], cos_pr, sin_pr, 512, KV_W, "proj_k")
        v_p = _proj_plain(h, wl["v"], 512, KV_W, "proj_v")
        ikw = _proj_ikw(h, wl["ikw"], cos_pr, sin_pr, wl["lnw"], wl["lnb"], 512)
        ik_p = ikw[:, :D_IDX]
        qt, iqt, vt, iwt = _proj_t(h, wl["q"].T, wl["iq"].T, wl["v"].T, wl["iw"].T, cos_pc, sin_pc, batch, seq, 512)
        gate_a = _proj_gate(h, wl["za"], wl["ga"], 512, 512, "proj_gate_a")
        gate_b = _proj_gate(h, wl["zb"], wl["gb"], 512, 512, "proj_gate_b")
        p_rw = _proj_plain(h, wl["rw"], 512, 640, "proj_rw")
        attn = _attn_prompt(k_bf.reshape(batch, seq, KV_W), vt, ik_p.astype(BF16).reshape(batch, seq, D_IDX),
                            qt, iqt, iwt, min(TOPK_MAX, seq // 4), 128, 512)
        rwo, wkv = _rwkv_branch(p_rw, jnp.zeros((batch, 1, N_SHIFT), F32),
                                jnp.zeros((batch, N_HEADS_B // 2, LANES, LANES), F32), rw_params, lnw, lnb,
                                batch, seq, 64, 512, seq // 512, BF16)
        xp = _merge(xp, attn.reshape(rows_p, C_A), rwo, gate_a, gate_b, wo, fw, 512, final)
        outs_p["k"].append(k_p.reshape(batch, seq, N_KV, HEAD_DIM))
        outs_p["v"].append(v_p.reshape(batch, seq, N_KV, HEAD_DIM))
        outs_p["ik"].append(ik_p.reshape(batch, seq, D_IDX))
        outs_p["wkv"].append(wkv)
        outs_p["sh"].append(p_rw.reshape(batch, seq, N_SHIFT)[:, -1])

        h = _rmsnorm(xs, norm_w[l], BF16, 512)
        w_qkiq = jnp.concatenate([wl["q"], wl["k"], wl["iq"]], axis=1)
        qki, _ = _proj_rope(h, w_qkiq, cos_sr, sin_sr, 512, 256, "proj_qkiq_s")
        q_s, k_s, iq_s = qki[:, :C_A], qki[:, C_A:C_A + KV_W], qki[:, C_A + KV_W:]
        v_s = _proj_plain(h, wl["v"], 512, KV_W, "proj_v_s")
        ikw = _proj_ikw(h, wl["ikw"], cos_sr, sin_sr, wl["lnw"], wl["lnb"], 512)
        ik_s, iw_s = ikw[:, :D_IDX], ikw[:, D_IDX:D_IDX + N_IDX_HEADS]
        gate_a = _proj_gate(h, wl["za"], wl["ga"], 512, 512, "proj_gate_a_s")
        gate_b = _proj_gate(h, wl["zb"], wl["gb"], 512, 512, "proj_gate_b_s")
        p_rw = _proj_plain(h, wl["rw"], 512, 640, "proj_rw_s")
        q4 = q_s.reshape(n_dec, n_new, N_HEADS_A, HEAD_DIM).transpose(0, 2, 1, 3)
        q2 = jnp.where(kv_onehot[None, :, None, :, None], q4[:, :, :, None, :], 0.0)
        q2 = q2.reshape(n_dec, N_HEADS_A * n_new, KV_W).astype(BF16)
        iq2 = iq_s.reshape(n_dec, n_new, N_IDX_HEADS, D_IDX).transpose(0, 2, 1, 3)
        iq2 = iq2.reshape(n_dec, N_IDX_HEADS * n_new, D_IDX).astype(BF16)
        iwc = iw_s.reshape(n_dec, n_new, N_IDX_HEADS).transpose(0, 2, 1).reshape(n_dec, N_IDX_HEADS * n_new, 1)
        o2 = _attn_sample(page_table, q2, iq2, iwc, k_s.reshape(n_dec, n_new, KV_W), v_s.reshape(n_dec, n_new, KV_W),
                          ik_s.reshape(n_dec, n_new, D_IDX), cache_k[l].reshape(n_pool, page, KV_W),
                          cache_v[l].reshape(n_pool, page, KV_W), cache_idx_k[l], min(TOPK_MAX, (past + n_new) // 4), 1024)
        o5 = o2.reshape(n_dec, N_HEADS_A, n_new, N_KV, HEAD_DIM)
        attn_s = o5[:, jnp.arange(N_HEADS_A), :, head_group, :]
        attn_s = attn_s.transpose(1, 2, 0, 3).reshape(rows_s, C_A)
        rwo, wkv = _rwkv_branch(p_rw, state_shift[l], _block_diag_state(state_wkv[l]), rw_params, lnw, lnb,
                                n_dec, n_new, n_new, 8 * n_new, None, F32)
        xs = _merge(xs, attn_s, rwo, gate_a, gate_b, wo, fw, 512, final)
        outs_s["k"].append(k_s.reshape(n_dec, n_new, N_KV, HEAD_DIM))
        outs_s["v"].append(v_s.reshape(n_dec, n_new, N_KV, HEAD_DIM))
        outs_s["ik"].append(ik_s.reshape(n_dec, n_new, D_IDX))
        outs_s["wkv"].append(wkv)
        outs_s["sh"].append(p_rw.reshape(n_dec, n_new, N_SHIFT)[:, -1])

    st = lambda d, n: jnp.stack(d[n])
    return (xp.reshape(batch, seq, D_MODEL), xs.reshape(n_dec, n_new, D_MODEL),
            st(outs_p, "k"), st(outs_p, "v"), st(outs_p, "ik"), st(outs_p, "wkv"), st(outs_p, "sh"),
            st(outs_s, "k"), st(outs_s, "v"), st(outs_s, "ik"), st(outs_s, "wkv"), st(outs_s, "sh"))
```

```python
import functools

import jax
import jax.numpy as jnp
from jax import lax
from jax.experimental import pallas as pl
from jax.experimental.pallas import tpu as pltpu

F32, BF16, I32 = jnp.float32, jnp.bfloat16, jnp.int32

D_MODEL = 1024
N_HEADS_A = 16
HEAD_DIM = 64
N_KV = 4
HEADS_PER_KV = N_HEADS_A // N_KV
C_A = N_HEADS_A * HEAD_DIM
KV_W = N_KV * HEAD_DIM
N_IDX_HEADS = 8
D_IDX = 64
IQ_W = N_IDX_HEADS * D_IDX
TOPK_MAX = 256
ROPE_THETA = 10000.0
HEAD_B = 64
N_HEADS_B = D_MODEL // HEAD_B
C_B = N_HEADS_B * HEAD_B
R_W = 64
R_A = 64
N_SHIFT = 3 * C_B + R_W + R_A
GN_EPS = 64e-5
RMS_EPS = 1e-6
LN_EPS = 1e-6
IW_SCALE = N_IDX_HEADS ** -0.5 * D_IDX ** -0.5
Q_SCALE = HEAD_DIM ** -0.5
LOG2E = 1.4426950408889634
QT_SCALE = Q_SCALE * LOG2E

_SEG_SIZES = (("q", C_A), ("k", KV_W), ("v", KV_W), ("iq", IQ_W), ("ik", D_IDX), ("iw", N_IDX_HEADS),
              ("za", C_A), ("rw", N_SHIFT), ("zb", C_B), ("ga", D_MODEL), ("gb", D_MODEL))

LANES = 128
SUBLANES = 8
HALF = HEAD_DIM // 2

ROW_TILE = 512
WIDE_ROW_TILE = 1024
GATE_COL_TILE = 512
RW_COL_TILE = 640
ATT_Q_TILE = 128
ATT_KEY_CHUNK = 512
SAMPLE_KEY_CHUNK = 1024
RWKV_CHUNK = 64
RWKV_PREP_ROWS = 512

INT_MIN = -(2 ** 31)
KEY_NEG_INF = 0x807FFFFF - 2 ** 32
M_INIT = -1e30


V7X_VMEM_BYTES = 64 << 20
VMEM_LIMIT_CAP = V7X_VMEM_BYTES * 7 // 8


def _nbytes(shape, dtype):
    n = jnp.dtype(dtype).itemsize
    for s in shape:
        n *= s
    return n


def _cp(sem, blocks=(), scratch=(), values=()):
    kw = dict(dimension_semantics=sem)
    need = 2 * sum(_nbytes(*b) for b in blocks) + sum(_nbytes(*s) for s in scratch) + sum(_nbytes(*v) for v in values)
    if need:
        kw["vmem_limit_bytes"] = min(VMEM_LIMIT_CAP, max(need, 16 << 20))
    return pltpu.CompilerParams(**kw)


def _dot(a, b):
    return jnp.dot(a, b, preferred_element_type=F32)


def _dot_nt(a, b):
    return lax.dot_general(a, b, (((1,), (1,)), ((), ())), preferred_element_type=F32)


def _dot_tn(a, b):
    return lax.dot_general(a, b, (((0,), (0,)), ((), ())), preferred_element_type=F32)


def _split(x, parts):
    out = []
    for _ in range(parts):
        hi = x.astype(BF16)
        out.append(hi)
        x = x - hi.astype(F32)
    return out


def _dot_exact_rhs(a, b_bf16, parts):
    acc = None
    for t in _split(a, parts):
        d = _dot(t, b_bf16)
        acc = d if acc is None else acc + d
    return acc


def _dot_exact_lhs(a_bf16, b, parts):
    acc = None
    for t in _split(b, parts):
        d = _dot(a_bf16, t)
        acc = d if acc is None else acc + d
    return acc


def _mm3(a, b):
    a_hi, a_lo = _split(a, 2)
    b_hi, b_lo = _split(b, 2)
    return _dot(a_hi, b_hi) + (_dot(a_hi, b_lo) + _dot(a_lo, b_hi))


def _iota(shape, dim):
    return lax.broadcasted_iota(I32, shape, dim)


def _head_ones():
    r = _iota((LANES, LANES), 0) // HEAD_B
    c = _iota((LANES, LANES), 1) // HEAD_B
    return jnp.where(r == c, 1.0, 0.0).astype(BF16)


def _head_sum(x, g):
    return _dot_exact_rhs(x, g, 2)


def _rmsnorm_body(x_ref, w_ref, o_ref):
    x = x_ref[...]
    inv = lax.rsqrt(jnp.mean(x * x, axis=-1, keepdims=True) + RMS_EPS)
    o_ref[...] = (x * inv * w_ref[...]).astype(o_ref.dtype)


def _rmsnorm(x2d, w, out_dtype, tm):
    rows, d = x2d.shape
    return pl.pallas_call(
        _rmsnorm_body,
        out_shape=jax.ShapeDtypeStruct((rows, d), out_dtype),
        grid=(rows // tm,),
        in_specs=[pl.BlockSpec((tm, d), lambda i: (i, 0)), pl.BlockSpec((1, d), lambda i: (0, 0))],
        out_specs=pl.BlockSpec((tm, d), lambda i: (i, 0)),
        compiler_params=_cp(("parallel",)),
        name="rmsnorm",
    )(x2d, w.reshape(1, d))


def _proj_plain_body(h_ref, w_ref, o_ref):
    o_ref[...] = _dot(h_ref[...], w_ref[...]).astype(o_ref.dtype)


def _proj_plain(h, w, tm, tn, name):
    rows, d = h.shape
    n = w.shape[1]
    return pl.pallas_call(
        _proj_plain_body,
        out_shape=jax.ShapeDtypeStruct((rows, n), F32),
        grid=(rows // tm, n // tn),
        in_specs=[pl.BlockSpec((tm, d), lambda i, j: (i, 0)), pl.BlockSpec((d, tn), lambda i, j: (0, j))],
        out_specs=pl.BlockSpec((tm, tn), lambda i, j: (i, j)),
        compiler_params=_cp(("parallel", "arbitrary")),
        name=name,
    )(h, w)


def _rope_rows(x, cos_t, sin_t):
    first = (_iota((x.shape[0], LANES), 1) % HEAD_DIM) < HALF
    outs = []
    for c in range(x.shape[1] // LANES):
        xc = x[:, c * LANES:(c + 1) * LANES]
        partner = jnp.where(first, pltpu.roll(xc, LANES - HALF, 1), pltpu.roll(xc, HALF, 1))
        outs.append(xc * cos_t + partner * sin_t)
    return outs[0] if len(outs) == 1 else jnp.concatenate(outs, axis=1)


def _proj_rope_body(h_ref, w_ref, cos_ref, sin_ref, o_ref, obf_ref):
    y = _rope_rows(_dot(h_ref[...], w_ref[...]), cos_ref[...], sin_ref[...])
    o_ref[...] = y
    obf_ref[...] = y.astype(BF16)


def _proj_rope(h, w, cos_t, sin_t, tm, tn, name):
    rows, d = h.shape
    n = w.shape[1]
    return pl.pallas_call(
        _proj_rope_body,
        out_shape=(jax.ShapeDtypeStruct((rows, n), F32), jax.ShapeDtypeStruct((rows, n), BF16)),
        grid=(rows // tm, n // tn),
        in_specs=[pl.BlockSpec((tm, d), lambda i, j: (i, 0)), pl.BlockSpec((d, tn), lambda i, j: (0, j)),
                  pl.BlockSpec((tm, LANES), lambda i, j: (i, 0)), pl.BlockSpec((tm, LANES), lambda i, j: (i, 0))],
        out_specs=(pl.BlockSpec((tm, tn), lambda i, j: (i, j)), pl.BlockSpec((tm, tn), lambda i, j: (i, j))),
        compiler_params=_cp(("parallel", "arbitrary")),
        name=name,
    )(h, w, cos_t, sin_t)


def _proj_ikw_body(h_ref, w_ref, cos_ref, sin_ref, lnw_ref, lnb_ref, o_ref):
    acc = _dot(h_ref[...], w_ref[...])
    lane = _iota(acc.shape, 1)
    is_ik = lane < D_IDX
    mu = jnp.sum(jnp.where(is_ik, acc, 0.0), axis=-1, keepdims=True) * (1.0 / D_IDX)
    dev = jnp.where(is_ik, acc - mu, 0.0)
    var = jnp.sum(dev * dev, axis=-1, keepdims=True) * (1.0 / D_IDX)
    y = dev * lax.rsqrt(var + LN_EPS) * lnw_ref[...] + lnb_ref[...]
    partner = jnp.where(lane < HALF, pltpu.roll(y, LANES - HALF, 1), pltpu.roll(y, HALF, 1))
    yr = y * cos_ref[...] + partner * sin_ref[...]
    o_ref[...] = jnp.where(is_ik, yr, jnp.where(lane < D_IDX + N_IDX_HEADS, acc * IW_SCALE, 0.0))


def _proj_ikw(h, w, cos_t, sin_t, lnw, lnb, tm):
    rows, d = h.shape
    row = lambda i: (i, 0)
    fixed = lambda i: (0, 0)
    return pl.pallas_call(
        _proj_ikw_body,
        out_shape=jax.ShapeDtypeStruct((rows, LANES), F32),
        grid=(rows // tm,),
        in_specs=[pl.BlockSpec((tm, d), row), pl.BlockSpec((d, LANES), fixed),
                  pl.BlockSpec((tm, LANES), row), pl.BlockSpec((tm, LANES), row),
                  pl.BlockSpec((1, LANES), fixed), pl.BlockSpec((1, LANES), fixed)],
        out_specs=pl.BlockSpec((tm, LANES), row),
        compiler_params=_cp(("parallel",)),
        name="proj_ikw",
    )(h, w, cos_t, sin_t, lnw, lnb)


def _proj_gate_body(h_ref, wz_ref, wg_ref, o_ref):
    h = h_ref[...]
    z = _dot(h, wz_ref[...])
    g = _dot(h, wg_ref[...])
    o_ref[...] = jax.nn.sigmoid(g) * (z * jax.nn.sigmoid(z))


def _proj_gate(h, wz, wg, tm, tn, name):
    rows, d = h.shape
    n = wz.shape[1]
    return pl.pallas_call(
        _proj_gate_body,
        out_shape=jax.ShapeDtypeStruct((rows, n), F32),
        grid=(rows // tm, n // tn),
        in_specs=[pl.BlockSpec((tm, d), lambda i, j: (i, 0)), pl.BlockSpec((d, tn), lambda i, j: (0, j)),
                  pl.BlockSpec((d, tn), lambda i, j: (0, j))],
        out_specs=pl.BlockSpec((tm, tn), lambda i, j: (i, j)),
        compiler_params=_cp(("parallel", "arbitrary")),
        name=name,
    )(h, wz, wg)


def _rope_cols(x, c, s):
    outs = []
    for hh in range(x.shape[0] // HEAD_DIM):
        x0 = x[hh * HEAD_DIM:hh * HEAD_DIM + HALF]
        x1 = x[hh * HEAD_DIM + HALF:(hh + 1) * HEAD_DIM]
        outs.append(x0 * c - x1 * s)
        outs.append(x1 * c + x0 * s)
    return jnp.concatenate(outs, axis=0)


def _proj_t_body(h_ref, wq_ref, wiq_ref, wv_ref, wiw_ref, cos_ref, sin_ref, oq_ref, oiq_ref, ov_ref, oiw_ref):
    h = h_ref[...]
    c = cos_ref[...]
    s = sin_ref[...]
    oq_ref[0] = (_rope_cols(_dot_nt(wq_ref[...], h), c, s) * QT_SCALE).astype(BF16)
    oiq_ref[0] = _rope_cols(_dot_nt(wiq_ref[...], h), c, s).astype(BF16)
    ov_ref[0] = _dot_nt(wv_ref[...], h).astype(BF16)
    oiw_ref[0] = _dot_nt(wiw_ref[...], h) * IW_SCALE


def _proj_t(h, wq_t, wiq_t, wv_t, wiw_t, cos_c, sin_c, batch, seq, tm):
    d = h.shape[1]
    nblk = seq // tm
    fixed = lambda b, i: (0, 0)
    col = lambda b, i: (b, 0, i)
    return pl.pallas_call(
        _proj_t_body,
        out_shape=(jax.ShapeDtypeStruct((batch, C_A, seq), BF16), jax.ShapeDtypeStruct((batch, IQ_W, seq), BF16),
                   jax.ShapeDtypeStruct((batch, KV_W, seq), BF16), jax.ShapeDtypeStruct((batch, N_IDX_HEADS, seq), F32)),
        grid=(batch, nblk),
        in_specs=[pl.BlockSpec((tm, d), lambda b, i: (b * nblk + i, 0)),
                  pl.BlockSpec((C_A, d), fixed), pl.BlockSpec((IQ_W, d), fixed), pl.BlockSpec((KV_W, d), fixed),
                  pl.BlockSpec((N_IDX_HEADS, d), fixed),
                  pl.BlockSpec((HALF, tm), lambda b, i: (0, i)), pl.BlockSpec((HALF, tm), lambda b, i: (0, i))],
        out_specs=(pl.BlockSpec((1, C_A, tm), col), pl.BlockSpec((1, IQ_W, tm), col),
                   pl.BlockSpec((1, KV_W, tm), col), pl.BlockSpec((1, N_IDX_HEADS, tm), col)),
        compiler_params=_cp(("parallel", "parallel"),
                            blocks=[((tm, d), BF16), ((C_A + IQ_W + KV_W + N_IDX_HEADS, d), BF16),
                                    ((C_A + IQ_W + KV_W, tm), BF16)],
                            values=[((C_A + IQ_W + KV_W, tm), F32)] * 2),
        name="proj_transposed",
    )(h, wq_t, wiq_t, wv_t, wiw_t, cos_c, sin_c)


def _key_to_float(key):
    bits = key ^ ((key >> 31) & 0x7FFFFFFF)
    return lax.bitcast_convert_type(bits, F32)


def _kth_largest_key(count_ge, like, k, rounds):
    zero = jnp.zeros_like(like)
    (c0,) = count_ge([_key_to_float(zero)])
    t = jnp.where(c0 >= k, zero, jnp.full_like(like, INT_MIN))
    top = 31
    for width, n_rounds in rounds:
        def round_fn(i, t, top=top, width=width):
            shift = top - width * (i + 1)
            cands = [t | (jnp.int32(v) << shift) for v in range(1, 2 ** width)]
            counts = count_ge([_key_to_float(c) for c in cands])
            accepted = jnp.zeros_like(t)
            for c in counts:
                accepted = accepted + (c >= k).astype(I32)
            return t | (accepted << shift)

        t = lax.fori_loop(0, n_rounds, round_fn, t)
        top -= width * n_rounds
    assert top == 0
    return t


def _attn_prompt_body(k_ref, vt_ref, ik_ref, qt_ref, iqt_ref, iwt_ref, o_ref,
                      sc_scr, qbd_scr, iqc_scr, m_scr, l_scr, acc_scr, *, tq, tk, seq, topk):
    j = pl.program_id(1)
    nk = ((j + 1) * tq + tk - 1) // tk
    neg_inf = -jnp.inf

    for h in range(N_IDX_HEADS):
        iqc_scr[:, h * tq:(h + 1) * tq] = iqt_ref[0, h * D_IDX:(h + 1) * D_IDX, :]
    qbd_scr[...] = jnp.zeros(qbd_scr.shape, BF16)
    for h in range(N_HEADS_A):
        gp, hh = divmod(h, 2 * HEADS_PER_KV)
        gl = hh // HEADS_PER_KV
        qbd_scr[gp, gl * HEAD_DIM:(gl + 1) * HEAD_DIM, hh * tq:(hh + 1) * tq] = qt_ref[0, h * HEAD_DIM:(h + 1) * HEAD_DIM, :]
    iw = iwt_ref[0]
    t_idx = j * tq + _iota((tk, tq), 1)
    row = _iota((tk, tq), 0)

    def idx_chunk(c, carry):
        off = pl.multiple_of(c * tk, tk)
        dots = _dot(ik_ref[0, pl.ds(off, tk), :], iqc_scr[...])
        acc = jnp.zeros((tk, tq), F32)
        for h in range(N_IDX_HEADS):
            acc = acc + jnp.maximum(dots[:, h * tq:(h + 1) * tq], 0.0) * iw[h:h + 1, :]
        sc_scr[pl.ds(off, tk), :] = jnp.where(off + row <= t_idx, acc, neg_inf)
        return carry

    lax.fori_loop(0, nk, idx_chunk, 0)

    def count(preds):
        def body(c, accs):
            off = pl.multiple_of(c * tk, tk)
            v = sc_scr[pl.ds(off, tk), :]
            s = off + row
            return tuple(a + jnp.sum(p(v, s).astype(I32).reshape(tk // SUBLANES, SUBLANES, tq), axis=0)
                         for a, p in zip(accs, preds))
        accs = lax.fori_loop(0, nk, body, tuple(jnp.zeros((SUBLANES, tq), I32) for _ in preds))
        return [jnp.sum(a, axis=0, keepdims=True) for a in accs]

    def count_ge(cfs):
        return count([lambda v, s, cf=cf: v >= cf for cf in cfs])

    t_key = _kth_largest_key(count_ge, jnp.zeros((1, tq), I32), topk, [(1, 31)])
    no_thr = t_key <= KEY_NEG_INF
    t_f = jnp.where(no_thr, neg_inf, _key_to_float(t_key))
    n_gt, n_ge = count([lambda v, s: v > t_f, lambda v, s: v >= t_f])
    need = (n_ge > topk) & jnp.logical_not(no_thr)
    room = topk - n_gt
    nbits = (seq - 1).bit_length()

    def x_step(i, x):
        cand = x | (jnp.int32(1) << (nbits - 1 - i))
        (before,) = count([lambda v, s: (v == t_f) & (s < cand)])
        return jnp.where(before < room, cand, x)

    any_need = jnp.max(need.astype(I32))
    x_tie = lax.fori_loop(0, nbits * any_need, x_step, jnp.zeros((1, tq), I32))
    x_lim = jnp.where(no_thr, -1, jnp.where(need, x_tie, seq))

    m_scr[...] = jnp.full(m_scr.shape, M_INIT, F32)
    l_scr[...] = jnp.zeros(l_scr.shape, F32)
    acc_scr[...] = jnp.zeros(acc_scr.shape, F32)
    th = tk // 2
    ones_rows = jnp.ones((2 * SUBLANES, th), BF16)

    def att_chunk(c, carry):
        off = pl.multiple_of(c * tk, tk)
        v = sc_scr[pl.ds(off, tk), :]
        sel = (v > t_f) | ((v == t_f) & (off + row <= x_lim))
        bias = jnp.where(sel, 0.0, neg_inf)
        s8s = [_dot(k_ref[0, pl.ds(off, tk), gp * LANES:(gp + 1) * LANES], qbd_scr[gp]) for gp in range(N_KV // 2)]
        for half in range(2):
            rows = slice(half * th, (half + 1) * th)
            off_h = pl.multiple_of(off + half * th, th)
            for g in range(N_KV):
                gp, gl = divmod(g, 2)
                ps = []
                for hl in range(HEADS_PER_KV):
                    hh = gl * HEADS_PER_KV + hl
                    h = g * HEADS_PER_KV + hl
                    sh = s8s[gp][rows, hh * tq:(hh + 1) * tq] + bias[rows]
                    m_old = m_scr[h:h + 1, :]
                    m_new = jnp.maximum(m_old, jnp.max(sh, axis=0, keepdims=True))
                    m_scr[h:h + 1, :] = m_new
                    ps.append(jnp.exp2(sh - m_new).astype(BF16))
                    alpha = jnp.exp2(m_old - m_new)
                    l_scr[h:h + 1, :] = alpha * l_scr[h:h + 1, :]
                    acc_scr[h * HEAD_DIM:(h + 1) * HEAD_DIM, :] = acc_scr[h * HEAD_DIM:(h + 1) * HEAD_DIM, :] * alpha
                vals = jnp.concatenate([vt_ref[0, g * HEAD_DIM:(g + 1) * HEAD_DIM, pl.ds(off_h, th)], ones_rows], axis=0)
                pv = _dot(vals, jnp.concatenate(ps, axis=1))
                for hl in range(HEADS_PER_KV):
                    h = g * HEADS_PER_KV + hl
                    acc_scr[h * HEAD_DIM:(h + 1) * HEAD_DIM, :] += pv[0:HEAD_DIM, hl * tq:(hl + 1) * tq]
                    l_scr[h:h + 1, :] += pv[HEAD_DIM:HEAD_DIM + 1, hl * tq:(hl + 1) * tq]
        return carry

    lax.fori_loop(0, nk, att_chunk, 0)

    for h in range(N_HEADS_A):
        acc_scr[h * HEAD_DIM:(h + 1) * HEAD_DIM, :] = acc_scr[h * HEAD_DIM:(h + 1) * HEAD_DIM, :] / l_scr[h:h + 1, :]
    o_ref[0] = acc_scr[...].T


def _attn_prompt(k_bf, vt, ik_bf, qt, iqt, iwt, topk, tq, tk):
    batch, seq, _ = k_bf.shape
    full = lambda b, j: (b, 0, 0)
    col = lambda b, j: (b, 0, j)
    body = functools.partial(_attn_prompt_body, tq=tq, tk=tk, seq=seq, topk=topk)
    return pl.pallas_call(
        body,
        out_shape=jax.ShapeDtypeStruct((batch, seq, C_A), F32),
        grid=(batch, seq // tq),
        in_specs=[pl.BlockSpec((1, seq, KV_W), full), pl.BlockSpec((1, KV_W, seq), full),
                  pl.BlockSpec((1, seq, D_IDX), full),
                  pl.BlockSpec((1, C_A, tq), col), pl.BlockSpec((1, IQ_W, tq), col),
                  pl.BlockSpec((1, N_IDX_HEADS, tq), col)],
        out_specs=pl.BlockSpec((1, tq, C_A), lambda b, j: (b, j, 0)),
        scratch_shapes=[pltpu.VMEM((seq, tq), F32),
                        pltpu.VMEM((N_KV // 2, LANES, 2 * HEADS_PER_KV * tq), BF16),
                        pltpu.VMEM((D_IDX, N_IDX_HEADS * tq), BF16),
                        pltpu.VMEM((N_HEADS_A, tq), F32), pltpu.VMEM((N_HEADS_A, tq), F32),
                        pltpu.VMEM((C_A, tq), F32)],
        compiler_params=_cp(("parallel", "arbitrary"),
                            blocks=[((seq, 2 * KV_W), BF16), ((seq, LANES), BF16), ((C_A + IQ_W, tq), BF16), ((tq, C_A), F32)],
                            scratch=[((seq, tq), F32), ((2 * LANES, C_A), BF16), ((C_A, tq), F32)],
                            values=[((tk, C_A), F32)] * 3),
        name="attn_prompt",
    )(k_bf, vt, ik_bf, qt, iqt, iwt)


def _attn_sample_body(pt_ref, q2_ref, iq2_ref, iwc_ref, kn_ref, vn_ref, ikn_ref, ck_hbm, cv_hbm, cik_hbm, o_ref,
                      kbuf, vbuf, ikbuf, isc_scr, s_scr, sem, *, n_seq, n_pages, page, n_new, topk, tl):
    b = pl.program_id(0)
    slot = b % 2
    past = n_pages * page
    n_chunks = past // tl
    rows_q = N_HEADS_A * n_new
    rows_i = N_IDX_HEADS * n_new
    neg_inf = -jnp.inf

    def page_copies(seq_i, slot_i, p):
        pg = pt_ref[seq_i, p]
        dst = pl.ds(pl.multiple_of(p * page, page), page)
        return (pltpu.make_async_copy(ck_hbm.at[pg], kbuf.at[slot_i, :, dst], sem.at[slot_i, 0]),
                pltpu.make_async_copy(cv_hbm.at[pg], vbuf.at[slot_i, :, dst], sem.at[slot_i, 1]),
                pltpu.make_async_copy(cik_hbm.at[pg], ikbuf.at[slot_i, :, dst], sem.at[slot_i, 2]))

    def start_seq(seq_i, slot_i):
        def body(p, carry):
            for cp in page_copies(seq_i, slot_i, p):
                cp.start()
            return carry
        lax.fori_loop(0, n_pages, body, 0, unroll=8)

    def wait_seq(seq_i, slot_i):
        def body(p, carry):
            for cp in page_copies(seq_i, slot_i, p):
                cp.wait()
            return carry
        lax.fori_loop(0, n_pages, body, 0, unroll=8)

    @pl.when(b == 0)
    def _():
        start_seq(0, 0)

    @pl.when(b + 1 < n_seq)
    def _():
        start_seq(b + 1, 1 - slot)

    wait_seq(b, slot)

    iq2 = iq2_ref[0]
    iwc = jnp.broadcast_to(iwc_ref[0], (rows_i, tl))

    def head_sum(d):
        return jnp.sum(d.reshape(N_IDX_HEADS, n_new, d.shape[1]), axis=0)

    def idx_chunk(c, carry):
        off = pl.multiple_of(c * tl, tl)
        d = _dot(iq2, ikbuf[slot, :, pl.ds(off, tl)].astype(BF16))
        isc_scr[:, pl.ds(off, tl)] = head_sum(jnp.maximum(d, 0.0) * iwc)
        return carry

    lax.fori_loop(0, n_chunks, idx_chunk, 0)
    pad_rows = LANES - n_new
    ik_new = jnp.concatenate([ikn_ref[0], jnp.zeros((pad_rows, D_IDX), F32)], axis=0).astype(BF16)
    lane_id = _iota((n_new, LANES), 1)
    sc_new = head_sum(jnp.maximum(_dot_nt(iq2, ik_new), 0.0) * iwc[:, :LANES])
    isc_scr[:, past:past + LANES] = jnp.where(lane_id <= _iota((n_new, LANES), 0), sc_new, neg_inf)
    total = past + LANES
    lane_tl = _iota((n_new, tl), 1)

    def wide(x):
        return jnp.broadcast_to(x, (n_new, tl))

    def fit(xw, v):
        return xw if v.shape[1] == tl else xw[:, :v.shape[1]]

    def count(preds):
        def fold(acc, hits):
            for t in range(hits.shape[1] // LANES):
                acc = acc + hits[:, t * LANES:(t + 1) * LANES]
            return acc

        def body(c, accs):
            off = pl.multiple_of(c * tl, tl)
            v = isc_scr[:, pl.ds(off, tl)]
            s = off + lane_tl
            return tuple(fold(a, jnp.where(p(v, s), 1.0, 0.0)) for a, p in zip(accs, preds))

        accs = lax.fori_loop(0, n_chunks, body, tuple(jnp.zeros((n_new, LANES), F32) for _ in preds))
        v_n = isc_scr[:, past:past + LANES]
        s_n = past + lane_id
        return [jnp.sum(a + jnp.where(p(v_n, s_n), 1.0, 0.0), axis=1, keepdims=True) for a, p in zip(accs, preds)]

    def count_ge(cfs):
        return count([lambda v, s, cw=wide(cf): v >= fit(cw, v) for cf in cfs])

    kf = float(topk)
    t_key = _kth_largest_key(count_ge, jnp.zeros((n_new, 1), I32), kf, [(1, 1), (2, 15)])
    no_thr = t_key <= KEY_NEG_INF
    t_f = jnp.where(no_thr, neg_inf, _key_to_float(t_key))
    t_w = wide(t_f)
    n_gt, n_ge = count([lambda v, s: v > fit(t_w, v), lambda v, s: v >= fit(t_w, v)])
    need = (n_ge > kf) & jnp.logical_not(no_thr)
    room = kf - n_gt
    nbits = (total - 1).bit_length()

    def x_step(i, x):
        c_w = wide(x | (jnp.int32(1) << (nbits - 1 - i)))
        (before,) = count([lambda v, s: (v == fit(t_w, v)) & (s < fit(c_w, v))])
        return jnp.where(before < room, x | (jnp.int32(1) << (nbits - 1 - i)), x)

    any_need = jnp.max(need.astype(I32))
    x_tie = lax.fori_loop(0, nbits * any_need, x_step, jnp.zeros((n_new, 1), I32))
    x_w = wide(jnp.where(no_thr, -1, jnp.where(need, x_tie, total)))

    def sel_bias(v, s):
        sel = (v > fit(t_w, v)) | ((v == fit(t_w, v)) & (s <= fit(x_w, v)))
        b8 = jnp.where(sel, 0.0, neg_inf)
        return jnp.concatenate([b8] * N_HEADS_A, axis=0)

    q2 = q2_ref[0]

    def score_chunk(c, m):
        off = pl.multiple_of(c * tl, tl)
        s = _dot(q2, kbuf[slot, :, pl.ds(off, tl)].astype(BF16)) * Q_SCALE
        s = s + sel_bias(isc_scr[:, pl.ds(off, tl)], off + lane_tl)
        s_scr[:, pl.ds(off, tl)] = s
        return jnp.maximum(m, jnp.max(s, axis=1, keepdims=True))

    m = lax.fori_loop(0, n_chunks, score_chunk, jnp.full((rows_q, 1), M_INIT, F32))
    k_new = jnp.concatenate([kn_ref[0], jnp.zeros((pad_rows, KV_W), F32)], axis=0).astype(BF16)
    v_new = jnp.concatenate([vn_ref[0], jnp.zeros((pad_rows, KV_W), F32)], axis=0).astype(BF16)
    s_n = _dot_nt(q2, k_new) * Q_SCALE + sel_bias(isc_scr[:, past:past + LANES], past + lane_id)
    m = jnp.maximum(m, jnp.max(s_n, axis=1, keepdims=True))
    p_n = jnp.exp(s_n - m)
    l0 = jnp.sum(p_n, axis=1, keepdims=True)
    o0 = _dot(p_n.astype(BF16), v_new)
    m_w = jnp.broadcast_to(m, (rows_q, tl))

    def pv_chunk(c, carry):
        l, o = carry
        off = pl.multiple_of(c * tl, tl)
        p = jnp.exp(s_scr[:, pl.ds(off, tl)] - m_w)
        l = l + jnp.sum(p, axis=1, keepdims=True)
        o = o + _dot_nt(p.astype(BF16), vbuf[slot, :, pl.ds(off, tl)].astype(BF16))
        return l, o

    l, o = lax.fori_loop(0, n_chunks, pv_chunk, (l0, o0))
    o_ref[0] = o / l


def _attn_sample(page_table, q2, iq2, iwc, k_new, v_new, ik_new, cache_k, cache_v, cache_ik, topk, tl):
    n_seq, n_pages = page_table.shape
    _, _, page = cache_k.shape
    n_new = k_new.shape[1]
    past = n_pages * page
    rows_q = N_HEADS_A * n_new
    rows_i = N_IDX_HEADS * n_new
    per_seq = lambda b, pt: (b, 0, 0)
    body = functools.partial(_attn_sample_body, n_seq=n_seq, n_pages=n_pages, page=page, n_new=n_new, topk=topk, tl=tl)
    grid_spec = pltpu.PrefetchScalarGridSpec(
        num_scalar_prefetch=1,
        grid=(n_seq,),
        in_specs=[pl.BlockSpec((1, rows_q, KV_W), per_seq), pl.BlockSpec((1, rows_i, D_IDX), per_seq),
                  pl.BlockSpec((1, rows_i, 1), per_seq),
                  pl.BlockSpec((1, n_new, KV_W), per_seq), pl.BlockSpec((1, n_new, KV_W), per_seq),
                  pl.BlockSpec((1, n_new, D_IDX), per_seq),
                  pl.BlockSpec(memory_space=pl.ANY), pl.BlockSpec(memory_space=pl.ANY),
                  pl.BlockSpec(memory_space=pl.ANY)],
        out_specs=pl.BlockSpec((1, rows_q, KV_W), per_seq),
        scratch_shapes=[pltpu.VMEM((2, KV_W, past), F32), pltpu.VMEM((2, KV_W, past), F32),
                        pltpu.VMEM((2, D_IDX, past), F32),
                        pltpu.VMEM((n_new, past + LANES), F32), pltpu.VMEM((rows_q, past), F32),
                        pltpu.SemaphoreType.DMA((2, 3))],
    )
    return pl.pallas_call(
        body,
        out_shape=jax.ShapeDtypeStruct((n_seq, rows_q, KV_W), F32),
        grid_spec=grid_spec,
        compiler_params=_cp(("arbitrary",),
                            blocks=[((2 * rows_q, KV_W), F32)],
                            scratch=[((2, 2 * KV_W + D_IDX, past), F32), ((n_new + rows_q, past + LANES), F32)],
                            values=[((KV_W, tl), F32), ((rows_q, tl), F32)] * 2),
        name="attn_sample",
    )(page_table, q2, iq2, iwc, k_new, v_new, ik_new, cache_k, cache_v, cache_ik)


def _rwkv_prep_body(p_ref, halo_ref, first_ref, mu_ref, w0_ref, wup_ref, a0_ref, aup_ref, kk_ref, ka_ref, rk_ref,
                    at_ref, rt_ref, kh_ref, bh_ref, kb_ref, bb_ref, v_ref, bonus_ref, pc_ref,
                    *, tb, chunk, blocks_per_seq):
    p = p_ref[...]
    row = _iota((tb, 1), 0)
    rolled = pltpu.roll(p, 1, 0)
    if blocks_per_seq is None:
        n_seq = tb // chunk
        first = first_ref[...]
        expanded = jnp.broadcast_to(first[:, None, :], (n_seq, chunk, N_SHIFT)).reshape(tb, N_SHIFT)
        prev = jnp.where(row % chunk == 0, expanded, rolled)
    else:
        i = pl.program_id(0)
        starts_seq = (i % blocks_per_seq) == 0
        row0 = jnp.where(starts_seq, first_ref[0], halo_ref[SUBLANES - 1:SUBLANES, :])
        prev = jnp.where(row == 0, row0, rolled)
    xs = p + (prev - p) * mu_ref[...]
    r = xs[:, 0:C_B]
    k = xs[:, C_B:2 * C_B]
    v = xs[:, 2 * C_B:3 * C_B]
    wa = xs[:, 3 * C_B:N_SHIFT]
    lane = _iota((tb, LANES), 1)
    t = jnp.where(lane < R_W, jnp.tanh(wa), wa).astype(BF16)
    log_w = -jax.nn.softplus(-(w0_ref[...] + _dot(t, wup_ref[...]))) - 0.5
    ld = -jnp.exp(log_w)
    a_sig = jax.nn.sigmoid(a0_ref[...] + _dot(t, aup_ref[...]))
    g = _head_ones()
    kk = k * kk_ref[...]
    k2 = k * (1.0 + (a_sig - 1.0) * ka_ref[...])
    rk = r * k2 * rk_ref[...]
    tri = jnp.where(_iota((chunk, chunk), 1) <= _iota((chunk, chunk), 0), 1.0, 0.0).astype(BF16)
    for ch in range(tb // chunk):
        rows = slice(ch * chunk, (ch + 1) * chunk)
        ld_c = ld[rows]
        cum = _dot_exact_lhs(tri, ld_c, 3)
        tot = cum[chunk - 1:chunk]
        rem = tot - cum
        pc_ref[0, ch:ch + 1, :] = jnp.exp(tot)
        e_cum = jnp.exp(cum)
        e_inv = jnp.exp(-cum)
        e_rem = jnp.exp(rem)
        e_prev = jnp.exp(cum - ld_c)
        for c in range(C_B // LANES):
            sl = slice(c * LANES, (c + 1) * LANES)
            kk_c = kk[rows, sl]
            nrm = jnp.maximum(jnp.sqrt(_head_sum(kk_c * kk_c, g)), 1e-12)
            kkn = kk_c / nrm
            bv = kkn * a_sig[rows, sl]
            at_ref[rows, sl] = (kkn * e_prev[:, sl]).astype(at_ref.dtype)
            rt_ref[rows, sl] = (r[rows, sl] * e_cum[:, sl]).astype(rt_ref.dtype)
            kh_ref[rows, sl] = (k2[rows, sl] * e_inv[:, sl]).astype(kh_ref.dtype)
            bh_ref[rows, sl] = (bv * e_inv[:, sl]).astype(bh_ref.dtype)
            kb_ref[rows, sl] = (k2[rows, sl] * e_rem[:, sl]).astype(kb_ref.dtype)
            bb_ref[rows, sl] = (bv * e_rem[:, sl]).astype(bb_ref.dtype)
            v_ref[rows, sl] = v[rows, sl].astype(v_ref.dtype)
            bonus_ref[rows, sl] = _head_sum(rk[rows, sl], g) * v[rows, sl]


def _rwkv_prep(p_rw, first_prev, params, tb, chunk, blocks_per_seq, store_dtype):
    rows = p_rw.shape[0]
    mu, w0, wup, a0, aup, kk, ka, rk = params
    nblk = rows // tb
    n_chunks = tb // chunk
    rowb = lambda i: (i, 0)
    fixed = lambda i: (0, 0)
    if blocks_per_seq is None:
        halo_spec = pl.BlockSpec((SUBLANES, N_SHIFT), fixed)
        first_spec = pl.BlockSpec((n_chunks, N_SHIFT), rowb)
    else:
        halo_spec = pl.BlockSpec((SUBLANES, N_SHIFT), lambda i: (jnp.maximum(i * (tb // SUBLANES) - 1, 0), 0))
        first_spec = pl.BlockSpec((1, 1, N_SHIFT), lambda i: (i // blocks_per_seq, 0, 0))
    vec = lambda n: pl.BlockSpec((1, n), fixed)
    wide = jax.ShapeDtypeStruct((rows, C_B), store_dtype)
    body = functools.partial(_rwkv_prep_body, tb=tb, chunk=chunk, blocks_per_seq=blocks_per_seq)
    return pl.pallas_call(
        body,
        out_shape=(wide,) * 7 + (jax.ShapeDtypeStruct((rows, C_B), F32),
                                 jax.ShapeDtypeStruct((nblk, n_chunks, C_B), F32)),
        grid=(nblk,),
        in_specs=[pl.BlockSpec((tb, N_SHIFT), rowb), halo_spec, first_spec, vec(N_SHIFT), vec(C_B),
                  pl.BlockSpec((LANES, C_B), fixed), vec(C_B), pl.BlockSpec((LANES, C_B), fixed),
                  vec(C_B), vec(C_B), vec(C_B)],
        out_specs=(pl.BlockSpec((tb, C_B), rowb),) * 8 + (pl.BlockSpec((1, n_chunks, C_B), lambda i: (i, 0, 0)),),
        compiler_params=_cp(("parallel",),
                            blocks=[((tb, N_SHIFT), F32), ((tb, 7 * C_B), store_dtype), ((tb, C_B), F32), ((2 * LANES, C_B), BF16)],
                            values=[((tb, N_SHIFT), F32)] * 4),
        name="rwkv_prep",
    )(p_rw, p_rw, first_prev, mu, w0, wup, a0, aup, kk, ka, rk)


def _rwkv_scan_body(at_ref, rt_ref, kh_ref, bh_ref, kb_ref, bb_ref, v_ref, bonus_ref, pc_ref, s0_ref, lnw_ref, lnb_ref,
                    y_ref, sout_ref, s_scr, *, chunk, n_chunks):
    c = pl.program_id(1)
    two = 2 * chunk

    @pl.when(c == 0)
    def _():
        s_scr[...] = s0_ref[0]

    ri = _iota((two, two), 0)
    ci = _iota((two, two), 1)
    same = (ri // chunk) == (ci // chunk)
    strict = same & (ci < ri)
    incl = same & (ci <= ri)
    stack_mask = (_iota((two, LANES), 0) // chunk) == (_iota((two, LANES), 1) // HEAD_B)
    eye = jnp.where(ri == ci, 1.0, 0.0)
    g = _head_ones()
    n_factors = (chunk - 1).bit_length()

    def stack(ref, sl):
        x = ref[0, :, sl].astype(BF16)
        return jnp.where(stack_mask, jnp.concatenate([x, x], axis=0), jnp.zeros((two, LANES), BF16))

    def mm(x, y):
        return _dot(x.astype(BF16), y.astype(BF16))

    pairs = range(N_HEADS_B // 2)
    sls = [slice(pr * LANES, (pr + 1) * LANES) for pr in pairs]
    a_s = [stack(at_ref, sl) for sl in sls]
    b_s = [stack(bh_ref, sl) for sl in sls]
    m_ab = [jnp.where(strict, _dot_nt(a_s[p], b_s[p]), 0.0) for p in pairs]
    pw = [-m for m in m_ab]
    t_inv = [eye + x for x in pw]
    for _ in range(n_factors - 1):
        pw = [mm(x, x) for x in pw]
        t_inv = [t + mm(t, x) for t, x in zip(t_inv, pw)]
    k_s = [stack(kh_ref, sl) for sl in sls]
    r_s = [stack(rt_ref, sl) for sl in sls]
    v_s = [stack(v_ref, sl) for sl in sls]
    m_ak = [jnp.where(strict, _dot_nt(a_s[p], k_s[p]), 0.0).astype(BF16) for p in pairs]
    m_rb = [jnp.where(incl, _dot_nt(r_s[p], b_s[p]), 0.0).astype(BF16) for p in pairs]
    m_rk = [jnp.where(incl, _dot_nt(r_s[p], k_s[p]), 0.0).astype(BF16) for p in pairs]
    s_old = [s_scr[p] for p in pairs]
    s_bf = [s.astype(BF16) for s in s_old]
    w = [_dot_nt(a_s[p], s_bf[p]) + _dot(m_ak[p], v_s[p]) for p in pairs]
    u_bf = [(-mm(t_inv[p], w[p])).astype(BF16) for p in pairs]
    y2 = [_dot_nt(r_s[p], s_bf[p]) + _dot(m_rk[p], v_s[p]) + _dot(m_rb[p], u_bf[p]) for p in pairs]
    kb_s = [stack(kb_ref, sl) for sl in sls]
    bb_s = [stack(bb_ref, sl) for sl in sls]
    for p in pairs:
        s_scr[p] = s_old[p] * pc_ref[0, 0, :, sls[p]] + _dot_tn(v_s[p], kb_s[p]) + _dot_tn(u_bf[p], bb_s[p])
    ys = [y[0:chunk] + y[chunk:two] for y in y2]
    means = [mm(y, g) * (1.0 / HEAD_B) for y in ys]
    devs = [y - m for y, m in zip(ys, means)]
    vars_ = [mm(d * d, g) * (1.0 / HEAD_B) for d in devs]
    for p in pairs:
        sl = sls[p]
        y_ref[0, :, sl] = devs[p] * lax.rsqrt(vars_[p] + GN_EPS) * lnw_ref[:, sl] + lnb_ref[:, sl] + bonus_ref[0, :, sl]

    @pl.when(c == n_chunks - 1)
    def _():
        sout_ref[0] = s_scr[...]


def _rwkv_scan(prep, s0_bd, lnw, lnb, n_seq, n_chunks, chunk):
    at, rt, kh, bh, kb, bb, vv, bonus, pc = prep
    shp = lambda a: a.reshape(n_seq, n_chunks * chunk, C_B)
    pc4 = pc.reshape(n_seq, n_chunks, 1, C_B)
    tok = pl.BlockSpec((1, chunk, C_B), lambda b, c: (b, c, 0))
    state = pl.BlockSpec((1, N_HEADS_B // 2, LANES, LANES), lambda b, c: (b, 0, 0, 0))
    vec = pl.BlockSpec((1, C_B), lambda b, c: (0, 0))
    body = functools.partial(_rwkv_scan_body, chunk=chunk, n_chunks=n_chunks)
    return pl.pallas_call(
        body,
        out_shape=(jax.ShapeDtypeStruct((n_seq, n_chunks * chunk, C_B), F32),
                   jax.ShapeDtypeStruct((n_seq, N_HEADS_B // 2, LANES, LANES), F32)),
        grid=(n_seq, n_chunks),
        in_specs=[tok] * 8 + [pl.BlockSpec((1, 1, 1, C_B), lambda b, c: (b, c, 0, 0)), state, vec, vec],
        out_specs=(tok, state),
        scratch_shapes=[pltpu.VMEM((N_HEADS_B // 2, LANES, LANES), F32)],
        compiler_params=_cp(("parallel", "arbitrary"),
                            blocks=[((chunk, 9 * C_B), F32), ((N_HEADS_B, LANES, LANES), F32)],
                            scratch=[((N_HEADS_B // 2, LANES, LANES), F32)],
                            values=[((N_HEADS_B // 2, 2 * chunk, LANES), F32)] * 24),
        name="rwkv_scan",
    )(shp(at), shp(rt), shp(kh), shp(bh), shp(kb), shp(bb), shp(vv), shp(bonus), pc4, s0_bd, lnw, lnb)


def _merge_body(x_ref, attn_ref, rwo_ref, ga_ref, gb_ref, wo_ref, fw_ref, o_ref, *, final):
    merged = ga_ref[...] * attn_ref[...] + gb_ref[...] * rwo_ref[...]
    xn = x_ref[...] + _dot(merged.astype(BF16), wo_ref[...])
    if final:
        xn = xn * lax.rsqrt(jnp.mean(xn * xn, axis=-1, keepdims=True) + RMS_EPS) * fw_ref[...]
    o_ref[...] = xn


def _merge(x2d, attn, rwo, gate_a, gate_b, wo, fw, tm, final):
    rows, d = x2d.shape
    rowb = lambda i: (i, 0)
    fixed = lambda i: (0, 0)
    return pl.pallas_call(
        functools.partial(_merge_body, final=final),
        out_shape=jax.ShapeDtypeStruct((rows, d), F32),
        grid=(rows // tm,),
        in_specs=[pl.BlockSpec((tm, d), rowb)] * 5 + [pl.BlockSpec((d, d), fixed), pl.BlockSpec((1, d), fixed)],
        out_specs=pl.BlockSpec((tm, d), rowb),
        compiler_params=_cp(("parallel",), blocks=[((tm, 6 * d), F32), ((d, d), BF16)], values=[((tm, d), F32)] * 2),
        name="merge_out",
    )(x2d, attn, rwo, gate_a, gate_b, wo, fw)


def _rope_tables(pos):
    inv = jnp.power(ROPE_THETA, -jnp.arange(HALF, dtype=F32) / HALF)
    ang = pos.astype(F32)[:, None] * inv[None, :]
    return jnp.cos(ang), jnp.sin(ang)


def _row_tables(cos, sin):
    return jnp.tile(cos, (1, LANES // HALF)), jnp.tile(jnp.concatenate([-sin, sin], axis=1), (1, LANES // HEAD_DIM))


def _pad_rows_to(x, rows, at):
    out = jnp.zeros((rows,) + x.shape[1:], x.dtype)
    return lax.dynamic_update_slice_in_dim(out, x, at, axis=0)


def _block_diag_state(s):
    n = s.shape[0]
    s = s.reshape(n, N_HEADS_B // 2, 2, HEAD_B, HEAD_B)
    z = jnp.zeros_like(s[:, :, 0])
    top = jnp.concatenate([s[:, :, 0], z], axis=-1)
    bot = jnp.concatenate([z, s[:, :, 1]], axis=-1)
    return jnp.concatenate([top, bot], axis=-2)


def _diag_state(s_bd):
    n = s_bd.shape[0]
    a = s_bd[:, :, :HEAD_B, :HEAD_B]
    b = s_bd[:, :, HEAD_B:, HEAD_B:]
    return jnp.stack([a, b], axis=2).reshape(n, N_HEADS_B, HEAD_B, HEAD_B)


def _layer_weights(w_in_l, idx_k_ln_w_l, idx_k_ln_b_l, w_up_l, a_up_l):
    seg, off = {}, 0
    for name, size in _SEG_SIZES:
        seg[name] = w_in_l[:, off:off + size].astype(BF16)
        off += size
    pad = jnp.zeros((D_MODEL, LANES - D_IDX - N_IDX_HEADS), BF16)
    seg["ikw"] = jnp.concatenate([seg["ik"], seg["iw"], pad], axis=1)
    zeros = jnp.zeros((LANES - D_IDX,), F32)
    seg["lnw"] = jnp.concatenate([idx_k_ln_w_l, zeros]).reshape(1, LANES)
    seg["lnb"] = jnp.concatenate([idx_k_ln_b_l, zeros]).reshape(1, LANES)
    seg["wup"] = _pad_rows_to(w_up_l.astype(BF16), LANES, 0)
    seg["aup"] = _pad_rows_to(a_up_l.astype(BF16), LANES, R_W)
    return seg


def _rwkv_branch(p_rw, first_prev, s0_bd, rw_params, lnw, lnb, n_seq, seq_len, chunk, tb, blocks_per_seq, store_dtype):
    prep = _rwkv_prep(p_rw, first_prev, rw_params, tb, chunk, blocks_per_seq, store_dtype)
    y, s_bd = _rwkv_scan(prep, s0_bd, lnw, lnb, n_seq, seq_len // chunk, chunk)
    return y.reshape(n_seq * seq_len, C_B), _diag_state(s_bd)


def kernel(x_prompt, x_sample, cache_k, cache_v, cache_idx_k, state_wkv, state_shift, page_table, norm_w, w_in, idx_k_ln_w, idx_k_ln_b, mu_shift, w0, w_up, a0, a_up, k_k, k_a, r_k, ln_x_w, ln_x_b, w_o, final_norm_w):
    batch, seq, _ = x_prompt.shape
    n_dec, n_new, _ = x_sample.shape
    depth = w_in.shape[0]
    n_pool, page = cache_k.shape[1], cache_k.shape[2]
    n_pages = page_table.shape[1]
    past = n_pages * page
    rows_p, rows_s = batch * seq, n_dec * n_new

    cos_p, sin_p = _rope_tables(jnp.arange(seq))
    cos_s, sin_s = _rope_tables(past + jnp.arange(n_new))
    cos_s, sin_s = jnp.tile(cos_s, (n_dec, 1)), jnp.tile(sin_s, (n_dec, 1))
    cos_pr, sin_pr = _row_tables(jnp.tile(cos_p, (batch, 1)), jnp.tile(sin_p, (batch, 1)))
    cos_sr, sin_sr = _row_tables(cos_s, sin_s)
    cos_pc, sin_pc = cos_p.T, sin_p.T

    xp = x_prompt.reshape(rows_p, D_MODEL)
    xs = x_sample.reshape(rows_s, D_MODEL)
    outs_p = {n: [] for n in ("k", "v", "ik", "wkv", "sh")}
    outs_s = {n: [] for n in ("k", "v", "ik", "wkv", "sh")}
    head_group = jnp.arange(N_HEADS_A) // HEADS_PER_KV
    kv_onehot = head_group[:, None] == jnp.arange(N_KV)[None, :]

    for l in range(depth):
        wl = _layer_weights(w_in[l], idx_k_ln_w[l], idx_k_ln_b[l], w_up[l], a_up[l])
        vec = lambda a: a.reshape(1, -1)
        rw_params = (vec(mu_shift[l]), vec(w0[l]), wl["wup"], vec(a0[l]), wl["aup"], vec(k_k[l]), vec(k_a[l]), vec(r_k[l]))
        lnw, lnb = vec(ln_x_w[l]), vec(ln_x_b[l])
        wo = w_o[l].astype(BF16)
        final = l == depth - 1
        fw = vec(final_norm_w)

        h = _rmsnorm(xp, norm_w[l], BF16, ROW_TILE)
        k_p, k_bf = _proj_rope(h, wl["k"], cos_pr, sin_pr, ROW_TILE, KV_W, "proj_k")
        v_p = _proj_plain(h, wl["v"], ROW_TILE, KV_W, "proj_v")
        ikw = _proj_ikw(h, wl["ikw"], cos_pr, sin_pr, wl["lnw"], wl["lnb"], ROW_TILE)
        ik_p = ikw[:, :D_IDX]
        qt, iqt, vt, iwt = _proj_t(h, wl["q"].T, wl["iq"].T, wl["v"].T, wl["iw"].T, cos_pc, sin_pc, batch, seq, ROW_TILE)
        gate_a = _proj_gate(h, wl["za"], wl["ga"], WIDE_ROW_TILE, GATE_COL_TILE, "proj_gate_a")
        gate_b = _proj_gate(h, wl["zb"], wl["gb"], WIDE_ROW_TILE, GATE_COL_TILE, "proj_gate_b")
        p_rw = _proj_plain(h, wl["rw"], WIDE_ROW_TILE, RW_COL_TILE, "proj_rw")
        attn = _attn_prompt(k_bf.reshape(batch, seq, KV_W), vt, ik_p.astype(BF16).reshape(batch, seq, D_IDX),
                            qt, iqt, iwt, min(TOPK_MAX, seq // 4), ATT_Q_TILE, ATT_KEY_CHUNK)
        rwo, wkv = _rwkv_branch(p_rw, jnp.zeros((batch, 1, N_SHIFT), F32),
                                jnp.zeros((batch, N_HEADS_B // 2, LANES, LANES), F32), rw_params, lnw, lnb,
                                batch, seq, RWKV_CHUNK, RWKV_PREP_ROWS, seq // RWKV_PREP_ROWS, BF16)
        xp = _merge(xp, attn.reshape(rows_p, C_A), rwo, gate_a, gate_b, wo, fw, ROW_TILE, final)
        outs_p["k"].append(k_p.reshape(batch, seq, N_KV, HEAD_DIM))
        outs_p["v"].append(v_p.reshape(batch, seq, N_KV, HEAD_DIM))
        outs_p["ik"].append(ik_p.reshape(batch, seq, D_IDX))
        outs_p["wkv"].append(wkv)
        outs_p["sh"].append(p_rw.reshape(batch, seq, N_SHIFT)[:, -1])

        h = _rmsnorm(xs, norm_w[l], BF16, ROW_TILE)
        w_qkiq = jnp.concatenate([wl["q"], wl["k"], wl["iq"]], axis=1)
        qki, _ = _proj_rope(h, w_qkiq, cos_sr, sin_sr, ROW_TILE, KV_W, "proj_qkiq_s")
        q_s, k_s, iq_s = qki[:, :C_A], qki[:, C_A:C_A + KV_W], qki[:, C_A + KV_W:]
        v_s = _proj_plain(h, wl["v"], ROW_TILE, KV_W, "proj_v_s")
        ikw = _proj_ikw(h, wl["ikw"], cos_sr, sin_sr, wl["lnw"], wl["lnb"], ROW_TILE)
        ik_s, iw_s = ikw[:, :D_IDX], ikw[:, D_IDX:D_IDX + N_IDX_HEADS]
        gate_a = _proj_gate(h, wl["za"], wl["ga"], ROW_TILE, GATE_COL_TILE, "proj_gate_a_s")
        gate_b = _proj_gate(h, wl["zb"], wl["gb"], ROW_TILE, GATE_COL_TILE, "proj_gate_b_s")
        p_rw = _proj_plain(h, wl["rw"], ROW_TILE, RW_COL_TILE, "proj_rw_s")
        q4 = q_s.reshape(n_dec, n_new, N_HEADS_A, HEAD_DIM).transpose(0, 2, 1, 3)
        q2 = jnp.where(kv_onehot[None, :, None, :, None], q4[:, :, :, None, :], 0.0)
        q2 = q2.reshape(n_dec, N_HEADS_A * n_new, KV_W).astype(BF16)
        iq2 = iq_s.reshape(n_dec, n_new, N_IDX_HEADS, D_IDX).transpose(0, 2, 1, 3)
        iq2 = iq2.reshape(n_dec, N_IDX_HEADS * n_new, D_IDX).astype(BF16)
        iwc = iw_s.reshape(n_dec, n_new, N_IDX_HEADS).transpose(0, 2, 1).reshape(n_dec, N_IDX_HEADS * n_new, 1)
        ck_t = jnp.transpose(cache_k[l], (0, 2, 3, 1)).reshape(n_pool, KV_W, page)
        cv_t = jnp.transpose(cache_v[l], (0, 2, 3, 1)).reshape(n_pool, KV_W, page)
        cik_t = jnp.transpose(cache_idx_k[l], (0, 2, 1))
        o2 = _attn_sample(page_table, q2, iq2, iwc, k_s.reshape(n_dec, n_new, KV_W), v_s.reshape(n_dec, n_new, KV_W),
                          ik_s.reshape(n_dec, n_new, D_IDX), ck_t, cv_t, cik_t, min(TOPK_MAX, (past + n_new) // 4),
                          SAMPLE_KEY_CHUNK)
        o5 = o2.reshape(n_dec, N_HEADS_A, n_new, N_KV, HEAD_DIM)
        attn_s = o5[:, jnp.arange(N_HEADS_A), :, head_group, :]
        attn_s = attn_s.transpose(1, 2, 0, 3).reshape(rows_s, C_A)
        rwo, wkv = _rwkv_branch(p_rw, state_shift[l], _block_diag_state(state_wkv[l]), rw_params, lnw, lnb,
                                n_dec, n_new, n_new, SUBLANES * n_new, None, F32)
        xs = _merge(xs, attn_s, rwo, gate_a, gate_b, wo, fw, ROW_TILE, final)
        outs_s["k"].append(k_s.reshape(n_dec, n_new, N_KV, HEAD_DIM))
        outs_s["v"].append(v_s.reshape(n_dec, n_new, N_KV, HEAD_DIM))
        outs_s["ik"].append(ik_s.reshape(n_dec, n_new, D_IDX))
        outs_s["wkv"].append(wkv)
        outs_s["sh"].append(p_rw.reshape(n_dec, n_new, N_SHIFT)[:, -1])

    st = lambda d, n: jnp.stack(d[n])
    return (xp.reshape(batch, seq, D_MODEL), xs.reshape(n_dec, n_new, D_MODEL),
            st(outs_p, "k"), st(outs_p, "v"), st(outs_p, "ik"), st(outs_p, "wkv"), st(outs_p, "sh"),
            st(outs_s, "k"), st(outs_s, "v"), st(outs_s, "ik"), st(outs_s, "wkv"), st(outs_s, "sh"))
```

```python
import functools

import jax
import jax.numpy as jnp
from jax import lax
from jax.experimental import pallas as pl
from jax.experimental.pallas import tpu as pltpu

F32, BF16, I32 = jnp.float32, jnp.bfloat16, jnp.int32

D_MODEL = 1024
N_HEADS_A = 16
HEAD_DIM = 64
N_KV = 4
HEADS_PER_KV = N_HEADS_A // N_KV
C_A = N_HEADS_A * HEAD_DIM
KV_W = N_KV * HEAD_DIM
N_IDX_HEADS = 8
D_IDX = 64
IQ_W = N_IDX_HEADS * D_IDX
TOPK_MAX = 256
ROPE_THETA = 10000.0
HEAD_B = 64
N_HEADS_B = D_MODEL // HEAD_B
C_B = N_HEADS_B * HEAD_B
R_W = 64
R_A = 64
N_SHIFT = 3 * C_B + R_W + R_A
GN_EPS = 64e-5
RMS_EPS = 1e-6
LN_EPS = 1e-6
IW_SCALE = N_IDX_HEADS ** -0.5 * D_IDX ** -0.5
Q_SCALE = HEAD_DIM ** -0.5
LOG2E = 1.4426950408889634
QT_SCALE = Q_SCALE * LOG2E

_SEG_SIZES = (("q", C_A), ("k", KV_W), ("v", KV_W), ("iq", IQ_W), ("ik", D_IDX), ("iw", N_IDX_HEADS),
              ("za", C_A), ("rw", N_SHIFT), ("zb", C_B), ("ga", D_MODEL), ("gb", D_MODEL))

LANES = 128
SUBLANES = 8
HALF = HEAD_DIM // 2

ROW_TILE = 512
WIDE_ROW_TILE = 1024
GATE_COL_TILE = 512
RW_COL_TILE = 640
ATT_Q_TILE = 128
ATT_KEY_CHUNK = 512
SAMPLE_KEY_CHUNK = 1024
RWKV_CHUNK = 64
RWKV_PREP_ROWS = 512
BISECT_BLIND_ROUNDS = 22

INT_MIN = -(2 ** 31)
KEY_NEG_INF = 0x807FFFFF - 2 ** 32
M_INIT = -1e30


V7X_VMEM_BYTES = 64 << 20
VMEM_LIMIT_CAP = V7X_VMEM_BYTES * 7 // 8


def _nbytes(shape, dtype):
    n = jnp.dtype(dtype).itemsize
    for s in shape:
        n *= s
    return n


def _cp(sem, blocks=(), scratch=(), values=()):
    kw = dict(dimension_semantics=sem)
    need = 2 * sum(_nbytes(*b) for b in blocks) + sum(_nbytes(*s) for s in scratch) + sum(_nbytes(*v) for v in values)
    if need:
        kw["vmem_limit_bytes"] = min(VMEM_LIMIT_CAP, max(need, 16 << 20))
    return pltpu.CompilerParams(**kw)


def _dot(a, b):
    return jnp.dot(a, b, preferred_element_type=F32)


def _dot_nt(a, b):
    return lax.dot_general(a, b, (((1,), (1,)), ((), ())), preferred_element_type=F32)


def _dot_tn(a, b):
    return lax.dot_general(a, b, (((0,), (0,)), ((), ())), preferred_element_type=F32)


def _split(x, parts):
    out = []
    for _ in range(parts):
        hi = x.astype(BF16)
        out.append(hi)
        x = x - hi.astype(F32)
    return out


def _dot_exact_rhs(a, b_bf16, parts):
    acc = None
    for t in _split(a, parts):
        d = _dot(t, b_bf16)
        acc = d if acc is None else acc + d
    return acc


def _dot_exact_lhs(a_bf16, b, parts):
    acc = None
    for t in _split(b, parts):
        d = _dot(a_bf16, t)
        acc = d if acc is None else acc + d
    return acc


def _mm3(a, b):
    a_hi, a_lo = _split(a, 2)
    b_hi, b_lo = _split(b, 2)
    return _dot(a_hi, b_hi) + (_dot(a_hi, b_lo) + _dot(a_lo, b_hi))


def _iota(shape, dim):
    return lax.broadcasted_iota(I32, shape, dim)


def _head_ones():
    r = _iota((LANES, LANES), 0) // HEAD_B
    c = _iota((LANES, LANES), 1) // HEAD_B
    return jnp.where(r == c, 1.0, 0.0).astype(BF16)


def _head_sum(x, g):
    return _dot_exact_rhs(x, g, 2)


def _rmsnorm_body(x_ref, w_ref, o_ref):
    x = x_ref[...]
    inv = lax.rsqrt(jnp.mean(x * x, axis=-1, keepdims=True) + RMS_EPS)
    o_ref[...] = (x * inv * w_ref[...]).astype(o_ref.dtype)


def _rmsnorm(x2d, w, out_dtype, tm):
    rows, d = x2d.shape
    return pl.pallas_call(
        _rmsnorm_body,
        out_shape=jax.ShapeDtypeStruct((rows, d), out_dtype),
        grid=(rows // tm,),
        in_specs=[pl.BlockSpec((tm, d), lambda i: (i, 0)), pl.BlockSpec((1, d), lambda i: (0, 0))],
        out_specs=pl.BlockSpec((tm, d), lambda i: (i, 0)),
        compiler_params=_cp(("parallel",)),
        name="rmsnorm",
    )(x2d, w.reshape(1, d))


def _proj_plain_body(h_ref, w_ref, o_ref):
    o_ref[...] = _dot(h_ref[...], w_ref[...]).astype(o_ref.dtype)


def _proj_plain(h, w, tm, tn, name):
    rows, d = h.shape
    n = w.shape[1]
    return pl.pallas_call(
        _proj_plain_body,
        out_shape=jax.ShapeDtypeStruct((rows, n), F32),
        grid=(rows // tm, n // tn),
        in_specs=[pl.BlockSpec((tm, d), lambda i, j: (i, 0)), pl.BlockSpec((d, tn), lambda i, j: (0, j))],
        out_specs=pl.BlockSpec((tm, tn), lambda i, j: (i, j)),
        compiler_params=_cp(("parallel", "arbitrary")),
        name=name,
    )(h, w)


def _rope_rows(x, cos_t, sin_t):
    first = (_iota((x.shape[0], LANES), 1) % HEAD_DIM) < HALF
    outs = []
    for c in range(x.shape[1] // LANES):
        xc = x[:, c * LANES:(c + 1) * LANES]
        partner = jnp.where(first, pltpu.roll(xc, LANES - HALF, 1), pltpu.roll(xc, HALF, 1))
        outs.append(xc * cos_t + partner * sin_t)
    return outs[0] if len(outs) == 1 else jnp.concatenate(outs, axis=1)


def _proj_rope_body(h_ref, w_ref, cos_ref, sin_ref, o_ref, obf_ref):
    y = _rope_rows(_dot(h_ref[...], w_ref[...]), cos_ref[...], sin_ref[...])
    o_ref[...] = y
    obf_ref[...] = y.astype(BF16)


def _proj_rope(h, w, cos_t, sin_t, tm, tn, name):
    rows, d = h.shape
    n = w.shape[1]
    return pl.pallas_call(
        _proj_rope_body,
        out_shape=(jax.ShapeDtypeStruct((rows, n), F32), jax.ShapeDtypeStruct((rows, n), BF16)),
        grid=(rows // tm, n // tn),
        in_specs=[pl.BlockSpec((tm, d), lambda i, j: (i, 0)), pl.BlockSpec((d, tn), lambda i, j: (0, j)),
                  pl.BlockSpec((tm, LANES), lambda i, j: (i, 0)), pl.BlockSpec((tm, LANES), lambda i, j: (i, 0))],
        out_specs=(pl.BlockSpec((tm, tn), lambda i, j: (i, j)), pl.BlockSpec((tm, tn), lambda i, j: (i, j))),
        compiler_params=_cp(("parallel", "arbitrary")),
        name=name,
    )(h, w, cos_t, sin_t)


def _proj_ikw_body(h_ref, w_ref, cos_ref, sin_ref, lnw_ref, lnb_ref, o_ref):
    acc = _dot(h_ref[...], w_ref[...])
    lane = _iota(acc.shape, 1)
    is_ik = lane < D_IDX
    mu = jnp.sum(jnp.where(is_ik, acc, 0.0), axis=-1, keepdims=True) * (1.0 / D_IDX)
    dev = jnp.where(is_ik, acc - mu, 0.0)
    var = jnp.sum(dev * dev, axis=-1, keepdims=True) * (1.0 / D_IDX)
    y = dev * lax.rsqrt(var + LN_EPS) * lnw_ref[...] + lnb_ref[...]
    partner = jnp.where(lane < HALF, pltpu.roll(y, LANES - HALF, 1), pltpu.roll(y, HALF, 1))
    yr = y * cos_ref[...] + partner * sin_ref[...]
    o_ref[...] = jnp.where(is_ik, yr, jnp.where(lane < D_IDX + N_IDX_HEADS, acc * IW_SCALE, 0.0))


def _proj_ikw(h, w, cos_t, sin_t, lnw, lnb, tm):
    rows, d = h.shape
    row = lambda i: (i, 0)
    fixed = lambda i: (0, 0)
    return pl.pallas_call(
        _proj_ikw_body,
        out_shape=jax.ShapeDtypeStruct((rows, LANES), F32),
        grid=(rows // tm,),
        in_specs=[pl.BlockSpec((tm, d), row), pl.BlockSpec((d, LANES), fixed),
                  pl.BlockSpec((tm, LANES), row), pl.BlockSpec((tm, LANES), row),
                  pl.BlockSpec((1, LANES), fixed), pl.BlockSpec((1, LANES), fixed)],
        out_specs=pl.BlockSpec((tm, LANES), row),
        compiler_params=_cp(("parallel",)),
        name="proj_ikw",
    )(h, w, cos_t, sin_t, lnw, lnb)


def _proj_gate_body(h_ref, wz_ref, wg_ref, o_ref):
    h = h_ref[...]
    z = _dot(h, wz_ref[...])
    g = _dot(h, wg_ref[...])
    o_ref[...] = (jax.nn.sigmoid(g) * (z * jax.nn.sigmoid(z))).astype(o_ref.dtype)


def _proj_gate(h, wz, wg, tm, tn, name):
    rows, d = h.shape
    n = wz.shape[1]
    return pl.pallas_call(
        _proj_gate_body,
        out_shape=jax.ShapeDtypeStruct((rows, n), BF16),
        grid=(rows // tm, n // tn),
        in_specs=[pl.BlockSpec((tm, d), lambda i, j: (i, 0)), pl.BlockSpec((d, tn), lambda i, j: (0, j)),
                  pl.BlockSpec((d, tn), lambda i, j: (0, j))],
        out_specs=pl.BlockSpec((tm, tn), lambda i, j: (i, j)),
        compiler_params=_cp(("parallel", "arbitrary")),
        name=name,
    )(h, wz, wg)


def _rope_cols(x, c, s):
    outs = []
    for hh in range(x.shape[0] // HEAD_DIM):
        x0 = x[hh * HEAD_DIM:hh * HEAD_DIM + HALF]
        x1 = x[hh * HEAD_DIM + HALF:(hh + 1) * HEAD_DIM]
        outs.append(x0 * c - x1 * s)
        outs.append(x1 * c + x0 * s)
    return jnp.concatenate(outs, axis=0)


def _proj_t_body(h_ref, wq_ref, wiq_ref, wv_ref, wiw_ref, cos_ref, sin_ref, oq_ref, oiq_ref, ov_ref, oiw_ref):
    h = h_ref[...]
    c = cos_ref[...]
    s = sin_ref[...]
    oq_ref[0] = (_rope_cols(_dot_nt(wq_ref[...], h), c, s) * QT_SCALE).astype(BF16)
    oiq_ref[0] = _rope_cols(_dot_nt(wiq_ref[...], h), c, s).astype(BF16)
    ov_ref[0] = _dot_nt(wv_ref[...], h).astype(BF16)
    oiw_ref[0] = _dot_nt(wiw_ref[...], h) * IW_SCALE


def _proj_t(h, wq_t, wiq_t, wv_t, wiw_t, cos_c, sin_c, batch, seq, tm):
    d = h.shape[1]
    nblk = seq // tm
    fixed = lambda b, i: (0, 0)
    col = lambda b, i: (b, 0, i)
    return pl.pallas_call(
        _proj_t_body,
        out_shape=(jax.ShapeDtypeStruct((batch, C_A, seq), BF16), jax.ShapeDtypeStruct((batch, IQ_W, seq), BF16),
                   jax.ShapeDtypeStruct((batch, KV_W, seq), BF16), jax.ShapeDtypeStruct((batch, N_IDX_HEADS, seq), F32)),
        grid=(batch, nblk),
        in_specs=[pl.BlockSpec((tm, d), lambda b, i: (b * nblk + i, 0)),
                  pl.BlockSpec((C_A, d), fixed), pl.BlockSpec((IQ_W, d), fixed), pl.BlockSpec((KV_W, d), fixed),
                  pl.BlockSpec((N_IDX_HEADS, d), fixed),
                  pl.BlockSpec((HALF, tm), lambda b, i: (0, i)), pl.BlockSpec((HALF, tm), lambda b, i: (0, i))],
        out_specs=(pl.BlockSpec((1, C_A, tm), col), pl.BlockSpec((1, IQ_W, tm), col),
                   pl.BlockSpec((1, KV_W, tm), col), pl.BlockSpec((1, N_IDX_HEADS, tm), col)),
        compiler_params=_cp(("parallel", "parallel"),
                            blocks=[((tm, d), BF16), ((C_A + IQ_W + KV_W + N_IDX_HEADS, d), BF16),
                                    ((C_A + IQ_W + KV_W, tm), BF16)],
                            values=[((C_A + IQ_W + KV_W, tm), F32)] * 2),
        name="proj_transposed",
    )(h, wq_t, wiq_t, wv_t, wiw_t, cos_c, sin_c)


def _key_to_float(key):
    bits = key ^ ((key >> 31) & 0x7FFFFFFF)
    return lax.bitcast_convert_type(bits, F32)


def _float_to_key(x):
    bits = lax.bitcast_convert_type(x, I32)
    return bits ^ ((bits >> 31) & 0x7FFFFFFF)


def _kth_largest_bisect(count_ge, k_lo, k_hi, k):
    def halve(lo, hi):
        active = lo < hi
        mid = (lo >> 1) + (hi >> 1) + (((lo & 1) + (hi & 1) + 1) >> 1)
        (c,) = count_ge([_key_to_float(mid)])
        ge = c >= k
        new_lo = jnp.where(active & ge, mid, lo)
        new_hi = jnp.where(active, jnp.where(ge, jnp.where(c == k, mid, hi), mid - 1), hi)
        return new_lo, new_hi

    def open_rows(lo, hi):
        return jnp.max((lo < hi).astype(I32))

    lo, hi = lax.fori_loop(0, BISECT_BLIND_ROUNDS, lambda i, s: halve(*s), (k_lo, k_hi))

    def tail(state):
        lo, hi = halve(state[0], state[1])
        return lo, hi, open_rows(lo, hi)

    lo, _, _ = lax.while_loop(lambda s: s[2] > 0, tail, (lo, hi, open_rows(lo, hi)))
    return lo


def _kth_largest_key(count_ge, like, k, rounds):
    zero = jnp.zeros_like(like)
    (c0,) = count_ge([_key_to_float(zero)])
    t = jnp.where(c0 >= k, zero, jnp.full_like(like, INT_MIN))
    top = 31
    for width, n_rounds in rounds:
        def round_fn(i, t, top=top, width=width):
            shift = top - width * (i + 1)
            cands = [t | (jnp.int32(v) << shift) for v in range(1, 2 ** width)]
            counts = count_ge([_key_to_float(c) for c in cands])
            accepted = jnp.zeros_like(t)
            for c in counts:
                accepted = accepted + (c >= k).astype(I32)
            return t | (accepted << shift)

        t = lax.fori_loop(0, n_rounds, round_fn, t)
        top -= width * n_rounds
    assert top == 0
    return t


def _attn_prompt_body(k_ref, vt_ref, ik_ref, qt_ref, iqt_ref, iwt_ref, o_ref,
                      sc_scr, qbd_scr, iqc_scr, m_scr, l_scr, acc_scr, *, tq, tk, seq, topk):
    j = pl.program_id(1)
    nk = ((j + 1) * tq + tk - 1) // tk
    neg_inf = -jnp.inf

    for h in range(N_IDX_HEADS):
        iqc_scr[:, h * tq:(h + 1) * tq] = iqt_ref[0, h * D_IDX:(h + 1) * D_IDX, :]
    qbd_scr[...] = jnp.zeros(qbd_scr.shape, BF16)
    for h in range(N_HEADS_A):
        gp, hh = divmod(h, 2 * HEADS_PER_KV)
        gl = hh // HEADS_PER_KV
        qbd_scr[gp, gl * HEAD_DIM:(gl + 1) * HEAD_DIM, hh * tq:(hh + 1) * tq] = qt_ref[0, h * HEAD_DIM:(h + 1) * HEAD_DIM, :]
    iw = iwt_ref[0]
    t_idx = j * tq + _iota((tk, tq), 1)
    row = _iota((tk, tq), 0)

    def idx_chunk(c, carry):
        off = pl.multiple_of(c * tk, tk)
        dots = _dot(ik_ref[0, pl.ds(off, tk), :], iqc_scr[...])
        acc = jnp.zeros((tk, tq), F32)
        for h in range(N_IDX_HEADS):
            acc = acc + jnp.maximum(dots[:, h * tq:(h + 1) * tq], 0.0) * iw[h:h + 1, :]
        sc_scr[pl.ds(off, tk), :] = jnp.where(off + row <= t_idx, acc, neg_inf)
        return carry

    lax.fori_loop(0, nk, idx_chunk, 0)

    def count(preds):
        def body(c, accs):
            off = pl.multiple_of(c * tk, tk)
            v = sc_scr[pl.ds(off, tk), :]
            s = off + row
            return tuple(a + jnp.sum(p(v, s).astype(I32).reshape(tk // SUBLANES, SUBLANES, tq), axis=0)
                         for a, p in zip(accs, preds))
        accs = lax.fori_loop(0, nk, body, tuple(jnp.zeros((SUBLANES, tq), I32) for _ in preds))
        return [jnp.sum(a, axis=0, keepdims=True) for a in accs]

    def count_ge(cfs):
        return count([lambda v, s, cf=cf: v >= cf for cf in cfs])

    assert tk % topk == 0

    def class_max(c, best):
        off = pl.multiple_of(c * tk, tk)
        v = sc_scr[pl.ds(off, tk), :]
        for part in range(tk // topk):
            best = jnp.maximum(best, v[part * topk:(part + 1) * topk])
        return best

    best = lax.fori_loop(0, nk, class_max, jnp.full((topk, tq), neg_inf, F32))
    k_lo = _float_to_key(jnp.min(best, axis=0, keepdims=True))
    k_hi = _float_to_key(jnp.max(best, axis=0, keepdims=True))
    t_key = _kth_largest_bisect(count_ge, k_lo, k_hi, topk)
    no_thr = t_key <= KEY_NEG_INF
    t_f = jnp.where(no_thr, neg_inf, _key_to_float(t_key))
    n_gt, n_ge = count([lambda v, s: v > t_f, lambda v, s: v >= t_f])
    need = (n_ge > topk) & jnp.logical_not(no_thr)
    room = topk - n_gt
    nbits = (seq - 1).bit_length()

    def x_step(i, x):
        cand = x | (jnp.int32(1) << (nbits - 1 - i))
        (before,) = count([lambda v, s: (v == t_f) & (s < cand)])
        return jnp.where(before < room, cand, x)

    any_need = jnp.max(need.astype(I32))
    x_tie = lax.fori_loop(0, nbits * any_need, x_step, jnp.zeros((1, tq), I32))
    x_lim = jnp.where(no_thr, -1, jnp.where(need, x_tie, seq))

    m_scr[...] = jnp.full(m_scr.shape, M_INIT, F32)
    l_scr[...] = jnp.zeros(l_scr.shape, F32)
    acc_scr[...] = jnp.zeros(acc_scr.shape, F32)
    th = tk // 2
    ones_rows = jnp.ones((2 * SUBLANES, th), BF16)

    def att_chunk(c, carry):
        off = pl.multiple_of(c * tk, tk)
        v = sc_scr[pl.ds(off, tk), :]
        sel = (v > t_f) | ((v == t_f) & (off + row <= x_lim))
        bias = jnp.where(sel, 0.0, neg_inf)
        s8s = [_dot(k_ref[0, pl.ds(off, tk), gp * LANES:(gp + 1) * LANES], qbd_scr[gp]) for gp in range(N_KV // 2)]
        for half in range(2):
            rows = slice(half * th, (half + 1) * th)
            off_h = pl.multiple_of(off + half * th, th)
            bias_h = bias[rows]
            for g in range(N_KV):
                gp, gl = divmod(g, 2)
                ps, alphas = [], []
                for hl in range(HEADS_PER_KV):
                    hh = gl * HEADS_PER_KV + hl
                    h = g * HEADS_PER_KV + hl
                    sh = s8s[gp][rows, hh * tq:(hh + 1) * tq] + bias_h
                    m_old = m_scr[h:h + 1, :]
                    m_new = jnp.maximum(m_old, jnp.max(sh, axis=0, keepdims=True))
                    m_scr[h:h + 1, :] = m_new
                    ps.append(jnp.exp2(sh - m_new).astype(BF16))
                    alphas.append(jnp.exp2(m_old - m_new))
                vals = jnp.concatenate([vt_ref[0, g * HEAD_DIM:(g + 1) * HEAD_DIM, pl.ds(off_h, th)], ones_rows], axis=0)
                pv = _dot(vals, jnp.concatenate(ps, axis=1))
                for hl in range(HEADS_PER_KV):
                    h = g * HEADS_PER_KV + hl
                    hd = slice(h * HEAD_DIM, (h + 1) * HEAD_DIM)
                    cols = slice(hl * tq, (hl + 1) * tq)
                    acc_scr[hd, :] = acc_scr[hd, :] * alphas[hl] + pv[0:HEAD_DIM, cols]
                    l_scr[h:h + 1, :] = l_scr[h:h + 1, :] * alphas[hl] + pv[HEAD_DIM:HEAD_DIM + 1, cols]
        return carry

    lax.fori_loop(0, nk, att_chunk, 0)

    for h in range(N_HEADS_A):
        acc_scr[h * HEAD_DIM:(h + 1) * HEAD_DIM, :] = acc_scr[h * HEAD_DIM:(h + 1) * HEAD_DIM, :] / l_scr[h:h + 1, :]
    o_ref[0] = acc_scr[...].T.astype(o_ref.dtype)


def _attn_prompt(k_bf, vt, ik_bf, qt, iqt, iwt, topk, tq, tk):
    batch, seq, _ = k_bf.shape
    full = lambda b, j: (b, 0, 0)
    col = lambda b, j: (b, 0, j)
    body = functools.partial(_attn_prompt_body, tq=tq, tk=tk, seq=seq, topk=topk)
    return pl.pallas_call(
        body,
        out_shape=jax.ShapeDtypeStruct((batch, seq, C_A), BF16),
        grid=(batch, seq // tq),
        in_specs=[pl.BlockSpec((1, seq, KV_W), full), pl.BlockSpec((1, KV_W, seq), full),
                  pl.BlockSpec((1, seq, D_IDX), full),
                  pl.BlockSpec((1, C_A, tq), col), pl.BlockSpec((1, IQ_W, tq), col),
                  pl.BlockSpec((1, N_IDX_HEADS, tq), col)],
        out_specs=pl.BlockSpec((1, tq, C_A), lambda b, j: (b, j, 0)),
        scratch_shapes=[pltpu.VMEM((seq, tq), F32),
                        pltpu.VMEM((N_KV // 2, LANES, 2 * HEADS_PER_KV * tq), BF16),
                        pltpu.VMEM((D_IDX, N_IDX_HEADS * tq), BF16),
                        pltpu.VMEM((N_HEADS_A, tq), F32), pltpu.VMEM((N_HEADS_A, tq), F32),
                        pltpu.VMEM((C_A, tq), F32)],
        compiler_params=_cp(("parallel", "arbitrary"),
                            blocks=[((seq, 2 * KV_W), BF16), ((seq, LANES), BF16), ((C_A + IQ_W, tq), BF16), ((tq, C_A), F32)],
                            scratch=[((seq, tq), F32), ((2 * LANES, C_A), BF16), ((C_A, tq), F32)],
                            values=[((tk, C_A), F32)] * 3),
        name="attn_prompt",
    )(k_bf, vt, ik_bf, qt, iqt, iwt)


def _attn_sample_body(pt_ref, q2_ref, iq2_ref, iwc_ref, kn_ref, vn_ref, ikn_ref, ck_hbm, cv_hbm, cik_hbm, o_ref,
                      kbuf, vbuf, ikbuf, isc_scr, s_scr, sem, *, n_seq, n_pages, page, n_new, topk, tl):
    b = pl.program_id(0)
    slot = b % 2
    past = n_pages * page
    n_chunks = past // tl
    rows_q = N_HEADS_A * n_new
    rows_i = N_IDX_HEADS * n_new
    neg_inf = -jnp.inf

    def page_copies(seq_i, slot_i, p):
        pg = pt_ref[seq_i, p]
        dst = pl.ds(pl.multiple_of(p * page, page), page)
        return (pltpu.make_async_copy(ck_hbm.at[pg], kbuf.at[slot_i, :, dst], sem.at[slot_i, 0]),
                pltpu.make_async_copy(cv_hbm.at[pg], vbuf.at[slot_i, :, dst], sem.at[slot_i, 1]),
                pltpu.make_async_copy(cik_hbm.at[pg], ikbuf.at[slot_i, :, dst], sem.at[slot_i, 2]))

    def start_seq(seq_i, slot_i):
        def body(p, carry):
            for cp in page_copies(seq_i, slot_i, p):
                cp.start()
            return carry
        lax.fori_loop(0, n_pages, body, 0, unroll=8)

    def wait_seq(seq_i, slot_i):
        def body(p, carry):
            for cp in page_copies(seq_i, slot_i, p):
                cp.wait()
            return carry
        lax.fori_loop(0, n_pages, body, 0, unroll=8)

    @pl.when(b == 0)
    def _():
        start_seq(0, 0)

    @pl.when(b + 1 < n_seq)
    def _():
        start_seq(b + 1, 1 - slot)

    wait_seq(b, slot)

    iq2 = iq2_ref[0]
    iwc = jnp.broadcast_to(iwc_ref[0], (rows_i, tl))

    def head_sum(d):
        return jnp.sum(d.reshape(N_IDX_HEADS, n_new, d.shape[1]), axis=0)

    def idx_chunk(c, carry):
        off = pl.multiple_of(c * tl, tl)
        d = _dot(iq2, ikbuf[slot, :, pl.ds(off, tl)].astype(BF16))
        isc_scr[:, pl.ds(off, tl)] = head_sum(jnp.maximum(d, 0.0) * iwc)
        return carry

    lax.fori_loop(0, n_chunks, idx_chunk, 0)
    pad_rows = LANES - n_new
    ik_new = jnp.concatenate([ikn_ref[0], jnp.zeros((pad_rows, D_IDX), F32)], axis=0).astype(BF16)
    lane_id = _iota((n_new, LANES), 1)
    sc_new = head_sum(jnp.maximum(_dot_nt(iq2, ik_new), 0.0) * iwc[:, :LANES])
    isc_scr[:, past:past + LANES] = jnp.where(lane_id <= _iota((n_new, LANES), 0), sc_new, neg_inf)
    total = past + LANES
    lane_tl = _iota((n_new, tl), 1)

    def wide(x):
        return jnp.broadcast_to(x, (n_new, tl))

    def fit(xw, v):
        return xw if v.shape[1] == tl else xw[:, :v.shape[1]]

    def count(preds):
        def fold(acc, hits):
            for t in range(hits.shape[1] // LANES):
                acc = acc + hits[:, t * LANES:(t + 1) * LANES]
            return acc

        def body(c, accs):
            off = pl.multiple_of(c * tl, tl)
            v = isc_scr[:, pl.ds(off, tl)]
            s = off + lane_tl
            return tuple(fold(a, jnp.where(p(v, s), 1.0, 0.0)) for a, p in zip(accs, preds))

        accs = lax.fori_loop(0, n_chunks, body, tuple(jnp.zeros((n_new, LANES), F32) for _ in preds))
        v_n = isc_scr[:, past:past + LANES]
        s_n = past + lane_id
        return [jnp.sum(a + jnp.where(p(v_n, s_n), 1.0, 0.0), axis=1, keepdims=True) for a, p in zip(accs, preds)]

    def count_ge(cfs):
        return count([lambda v, s, cw=wide(cf): v >= fit(cw, v) for cf in cfs])

    kf = float(topk)
    t_key = _kth_largest_key(count_ge, jnp.zeros((n_new, 1), I32), kf, [(1, 1), (2, 15)])
    no_thr = t_key <= KEY_NEG_INF
    t_f = jnp.where(no_thr, neg_inf, _key_to_float(t_key))
    t_w = wide(t_f)
    n_gt, n_ge = count([lambda v, s: v > fit(t_w, v), lambda v, s: v >= fit(t_w, v)])
    need = (n_ge > kf) & jnp.logical_not(no_thr)
    room = kf - n_gt
    nbits = (total - 1).bit_length()

    def x_step(i, x):
        c_w = wide(x | (jnp.int32(1) << (nbits - 1 - i)))
        (before,) = count([lambda v, s: (v == fit(t_w, v)) & (s < fit(c_w, v))])
        return jnp.where(before < room, x | (jnp.int32(1) << (nbits - 1 - i)), x)

    any_need = jnp.max(need.astype(I32))
    x_tie = lax.fori_loop(0, nbits * any_need, x_step, jnp.zeros((n_new, 1), I32))
    x_w = wide(jnp.where(no_thr, -1, jnp.where(need, x_tie, total)))

    def sel_bias(v, s):
        sel = (v > fit(t_w, v)) | ((v == fit(t_w, v)) & (s <= fit(x_w, v)))
        b8 = jnp.where(sel, 0.0, neg_inf)
        return jnp.concatenate([b8] * N_HEADS_A, axis=0)

    q2 = q2_ref[0]

    def score_chunk(c, m):
        off = pl.multiple_of(c * tl, tl)
        s = _dot(q2, kbuf[slot, :, pl.ds(off, tl)].astype(BF16)) * Q_SCALE
        s = s + sel_bias(isc_scr[:, pl.ds(off, tl)], off + lane_tl)
        s_scr[:, pl.ds(off, tl)] = s
        return jnp.maximum(m, jnp.max(s, axis=1, keepdims=True))

    m = lax.fori_loop(0, n_chunks, score_chunk, jnp.full((rows_q, 1), M_INIT, F32))
    k_new = jnp.concatenate([kn_ref[0], jnp.zeros((pad_rows, KV_W), F32)], axis=0).astype(BF16)
    v_new = jnp.concatenate([vn_ref[0], jnp.zeros((pad_rows, KV_W), F32)], axis=0).astype(BF16)
    s_n = _dot_nt(q2, k_new) * Q_SCALE + sel_bias(isc_scr[:, past:past + LANES], past + lane_id)
    m = jnp.maximum(m, jnp.max(s_n, axis=1, keepdims=True))
    p_n = jnp.exp(s_n - m)
    l0 = jnp.sum(p_n, axis=1, keepdims=True)
    o0 = _dot(p_n.astype(BF16), v_new)
    m_w = jnp.broadcast_to(m, (rows_q, tl))

    def pv_chunk(c, carry):
        l, o = carry
        off = pl.multiple_of(c * tl, tl)
        p = jnp.exp(s_scr[:, pl.ds(off, tl)] - m_w)
        l = l + jnp.sum(p, axis=1, keepdims=True)
        o = o + _dot_nt(p.astype(BF16), vbuf[slot, :, pl.ds(off, tl)].astype(BF16))
        return l, o

    l, o = lax.fori_loop(0, n_chunks, pv_chunk, (l0, o0))
    o_ref[0] = o / l


def _attn_sample(page_table, q2, iq2, iwc, k_new, v_new, ik_new, cache_k, cache_v, cache_ik, topk, tl):
    n_seq, n_pages = page_table.shape
    _, _, page = cache_k.shape
    n_new = k_new.shape[1]
    past = n_pages * page
    rows_q = N_HEADS_A * n_new
    rows_i = N_IDX_HEADS * n_new
    per_seq = lambda b, pt: (b, 0, 0)
    body = functools.partial(_attn_sample_body, n_seq=n_seq, n_pages=n_pages, page=page, n_new=n_new, topk=topk, tl=tl)
    grid_spec = pltpu.PrefetchScalarGridSpec(
        num_scalar_prefetch=1,
        grid=(n_seq,),
        in_specs=[pl.BlockSpec((1, rows_q, KV_W), per_seq), pl.BlockSpec((1, rows_i, D_IDX), per_seq),
                  pl.BlockSpec((1, rows_i, 1), per_seq),
                  pl.BlockSpec((1, n_new, KV_W), per_seq), pl.BlockSpec((1, n_new, KV_W), per_seq),
                  pl.BlockSpec((1, n_new, D_IDX), per_seq),
                  pl.BlockSpec(memory_space=pl.ANY), pl.BlockSpec(memory_space=pl.ANY),
                  pl.BlockSpec(memory_space=pl.ANY)],
        out_specs=pl.BlockSpec((1, rows_q, KV_W), per_seq),
        scratch_shapes=[pltpu.VMEM((2, KV_W, past), F32), pltpu.VMEM((2, KV_W, past), F32),
                        pltpu.VMEM((2, D_IDX, past), F32),
                        pltpu.VMEM((n_new, past + LANES), F32), pltpu.VMEM((rows_q, past), F32),
                        pltpu.SemaphoreType.DMA((2, 3))],
    )
    return pl.pallas_call(
        body,
        out_shape=jax.ShapeDtypeStruct((n_seq, rows_q, KV_W), F32),
        grid_spec=grid_spec,
        compiler_params=_cp(("arbitrary",),
                            blocks=[((2 * rows_q, KV_W), F32)],
                            scratch=[((2, 2 * KV_W + D_IDX, past), F32), ((n_new + rows_q, past + LANES), F32)],
                            values=[((KV_W, tl), F32), ((rows_q, tl), F32)] * 2),
        name="attn_sample",
    )(page_table, q2, iq2, iwc, k_new, v_new, ik_new, cache_k, cache_v, cache_ik)


def _rwkv_prep_body(p_ref, halo_ref, first_ref, mu_ref, w0_ref, wup_ref, a0_ref, aup_ref, kk_ref, ka_ref, rk_ref,
                    at_ref, rt_ref, kh_ref, bh_ref, kb_ref, bb_ref, v_ref, bonus_ref, pc_ref,
                    *, tb, chunk, blocks_per_seq):
    p = p_ref[...]
    row = _iota((tb, 1), 0)
    rolled = pltpu.roll(p, 1, 0)
    if blocks_per_seq is None:
        n_seq = tb // chunk
        first = first_ref[...]
        expanded = jnp.broadcast_to(first[:, None, :], (n_seq, chunk, N_SHIFT)).reshape(tb, N_SHIFT)
        prev = jnp.where(row % chunk == 0, expanded, rolled)
    else:
        i = pl.program_id(0)
        starts_seq = (i % blocks_per_seq) == 0
        row0 = jnp.where(starts_seq, first_ref[0], halo_ref[SUBLANES - 1:SUBLANES, :])
        prev = jnp.where(row == 0, row0, rolled)
    xs = p + (prev - p) * mu_ref[...]
    r = xs[:, 0:C_B]
    k = xs[:, C_B:2 * C_B]
    v = xs[:, 2 * C_B:3 * C_B]
    wa = xs[:, 3 * C_B:N_SHIFT]
    lane = _iota((tb, LANES), 1)
    t = jnp.where(lane < R_W, jnp.tanh(wa), wa).astype(BF16)
    log_w = -jax.nn.softplus(-(w0_ref[...] + _dot(t, wup_ref[...]))) - 0.5
    ld = -jnp.exp(log_w)
    a_sig = jax.nn.sigmoid(a0_ref[...] + _dot(t, aup_ref[...]))
    g = _head_ones()
    kk = k * kk_ref[...]
    k2 = k * (1.0 + (a_sig - 1.0) * ka_ref[...])
    rk = r * k2 * rk_ref[...]
    tri = jnp.where(_iota((chunk, chunk), 1) <= _iota((chunk, chunk), 0), 1.0, 0.0).astype(BF16)
    for ch in range(tb // chunk):
        rows = slice(ch * chunk, (ch + 1) * chunk)
        ld_c = ld[rows]
        cum = _dot_exact_lhs(tri, ld_c, 3)
        tot = cum[chunk - 1:chunk]
        rem = tot - cum
        pc_ref[0, ch:ch + 1, :] = jnp.exp(tot)
        e_cum = jnp.exp(cum)
        e_inv = jnp.exp(-cum)
        e_rem = jnp.exp(rem)
        e_prev = jnp.exp(cum - ld_c)
        for c in range(C_B // LANES):
            sl = slice(c * LANES, (c + 1) * LANES)
            kk_c = kk[rows, sl]
            nrm = jnp.maximum(jnp.sqrt(_head_sum(kk_c * kk_c, g)), 1e-12)
            kkn = kk_c / nrm
            bv = kkn * a_sig[rows, sl]
            at_ref[rows, sl] = (kkn * e_prev[:, sl]).astype(at_ref.dtype)
            rt_ref[rows, sl] = (r[rows, sl] * e_cum[:, sl]).astype(rt_ref.dtype)
            kh_ref[rows, sl] = (k2[rows, sl] * e_inv[:, sl]).astype(kh_ref.dtype)
            bh_ref[rows, sl] = (bv * e_inv[:, sl]).astype(bh_ref.dtype)
            kb_ref[rows, sl] = (k2[rows, sl] * e_rem[:, sl]).astype(kb_ref.dtype)
            bb_ref[rows, sl] = (bv * e_rem[:, sl]).astype(bb_ref.dtype)
            v_ref[rows, sl] = v[rows, sl].astype(v_ref.dtype)
            bonus_ref[rows, sl] = _head_sum(rk[rows, sl], g) * v[rows, sl]


def _rwkv_prep(p_rw, first_prev, params, tb, chunk, blocks_per_seq, store_dtype):
    rows = p_rw.shape[0]
    mu, w0, wup, a0, aup, kk, ka, rk = params
    nblk = rows // tb
    n_chunks = tb // chunk
    rowb = lambda i: (i, 0)
    fixed = lambda i: (0, 0)
    if blocks_per_seq is None:
        halo_spec = pl.BlockSpec((SUBLANES, N_SHIFT), fixed)
        first_spec = pl.BlockSpec((n_chunks, N_SHIFT), rowb)
    else:
        halo_spec = pl.BlockSpec((SUBLANES, N_SHIFT), lambda i: (jnp.maximum(i * (tb // SUBLANES) - 1, 0), 0))
        first_spec = pl.BlockSpec((1, 1, N_SHIFT), lambda i: (i // blocks_per_seq, 0, 0))
    vec = lambda n: pl.BlockSpec((1, n), fixed)
    wide = jax.ShapeDtypeStruct((rows, C_B), store_dtype)
    body = functools.partial(_rwkv_prep_body, tb=tb, chunk=chunk, blocks_per_seq=blocks_per_seq)
    return pl.pallas_call(
        body,
        out_shape=(wide,) * 7 + (jax.ShapeDtypeStruct((rows, C_B), F32),
                                 jax.ShapeDtypeStruct((nblk, n_chunks, C_B), F32)),
        grid=(nblk,),
        in_specs=[pl.BlockSpec((tb, N_SHIFT), rowb), halo_spec, first_spec, vec(N_SHIFT), vec(C_B),
                  pl.BlockSpec((LANES, C_B), fixed), vec(C_B), pl.BlockSpec((LANES, C_B), fixed),
                  vec(C_B), vec(C_B), vec(C_B)],
        out_specs=(pl.BlockSpec((tb, C_B), rowb),) * 8 + (pl.BlockSpec((1, n_chunks, C_B), lambda i: (i, 0, 0)),),
        compiler_params=_cp(("parallel",),
                            blocks=[((tb, N_SHIFT), F32), ((tb, 7 * C_B), store_dtype), ((tb, C_B), F32), ((2 * LANES, C_B), BF16)],
                            values=[((tb, N_SHIFT), F32)] * 4),
        name="rwkv_prep",
    )(p_rw, p_rw, first_prev, mu, w0, wup, a0, aup, kk, ka, rk)


def _rwkv_scan_body(at_ref, rt_ref, kh_ref, bh_ref, kb_ref, bb_ref, v_ref, bonus_ref, pc_ref, s0_ref, lnw_ref, lnb_ref,
                    y_ref, sout_ref, s_scr, *, chunk, n_chunks):
    c = pl.program_id(1)
    two = 2 * chunk

    @pl.when(c == 0)
    def _():
        s_scr[...] = jnp.zeros(s_scr.shape, F32)
        for pr in range(N_HEADS_B // 2):
            s_scr[pr, 0:HEAD_B, 0:HEAD_B] = s0_ref[0, 2 * pr]
            s_scr[pr, HEAD_B:LANES, HEAD_B:LANES] = s0_ref[0, 2 * pr + 1]

    ri = _iota((two, two), 0)
    ci = _iota((two, two), 1)
    same = (ri // chunk) == (ci // chunk)
    strict = same & (ci < ri)
    incl = same & (ci <= ri)
    stack_mask = (_iota((two, LANES), 0) // chunk) == (_iota((two, LANES), 1) // HEAD_B)
    eye = jnp.where(ri == ci, 1.0, 0.0)
    g = _head_ones()
    n_factors = (chunk - 1).bit_length()

    def stack(ref, sl):
        x = ref[0, :, sl].astype(BF16)
        return jnp.where(stack_mask, jnp.concatenate([x, x], axis=0), jnp.zeros((two, LANES), BF16))

    def mm(x, y):
        return _dot(x.astype(BF16), y.astype(BF16))

    pairs = range(N_HEADS_B // 2)
    sls = [slice(pr * LANES, (pr + 1) * LANES) for pr in pairs]
    a_s = [stack(at_ref, sl) for sl in sls]
    b_s = [stack(bh_ref, sl) for sl in sls]
    m_ab = [jnp.where(strict, _dot_nt(a_s[p], b_s[p]), 0.0) for p in pairs]
    pw = [-m for m in m_ab]
    t_inv = [eye + x for x in pw]
    for _ in range(n_factors - 1):
        pw = [mm(x, x) for x in pw]
        t_inv = [t + mm(t, x) for t, x in zip(t_inv, pw)]
    k_s = [stack(kh_ref, sl) for sl in sls]
    r_s = [stack(rt_ref, sl) for sl in sls]
    v_s = [stack(v_ref, sl) for sl in sls]
    m_ak = [jnp.where(strict, _dot_nt(a_s[p], k_s[p]), 0.0).astype(BF16) for p in pairs]
    m_rb = [jnp.where(incl, _dot_nt(r_s[p], b_s[p]), 0.0).astype(BF16) for p in pairs]
    m_rk = [jnp.where(incl, _dot_nt(r_s[p], k_s[p]), 0.0).astype(BF16) for p in pairs]
    s_old = [s_scr[p] for p in pairs]
    s_bf = [s.astype(BF16) for s in s_old]
    w = [_dot_nt(a_s[p], s_bf[p]) + _dot(m_ak[p], v_s[p]) for p in pairs]
    u_bf = [(-mm(t_inv[p], w[p])).astype(BF16) for p in pairs]
    y2 = [_dot_nt(r_s[p], s_bf[p]) + _dot(m_rk[p], v_s[p]) + _dot(m_rb[p], u_bf[p]) for p in pairs]
    kb_s = [stack(kb_ref, sl) for sl in sls]
    bb_s = [stack(bb_ref, sl) for sl in sls]
    for p in pairs:
        s_scr[p] = s_old[p] * pc_ref[0, 0, :, sls[p]] + _dot_tn(v_s[p], kb_s[p]) + _dot_tn(u_bf[p], bb_s[p])
    ys = [y[0:chunk] + y[chunk:two] for y in y2]
    means = [mm(y, g) * (1.0 / HEAD_B) for y in ys]
    devs = [y - m for y, m in zip(ys, means)]
    vars_ = [mm(d * d, g) * (1.0 / HEAD_B) for d in devs]
    for p in pairs:
        sl = sls[p]
        y = devs[p] * lax.rsqrt(vars_[p] + GN_EPS) * lnw_ref[:, sl] + lnb_ref[:, sl] + bonus_ref[0, :, sl]
        y_ref[0, :, sl] = y.astype(y_ref.dtype)

    @pl.when(c == n_chunks - 1)
    def _():
        for pr in range(N_HEADS_B // 2):
            sout_ref[0, 2 * pr] = s_scr[pr, 0:HEAD_B, 0:HEAD_B]
            sout_ref[0, 2 * pr + 1] = s_scr[pr, HEAD_B:LANES, HEAD_B:LANES]


def _rwkv_scan(prep, s0, lnw, lnb, n_seq, n_chunks, chunk, y_dtype):
    at, rt, kh, bh, kb, bb, vv, bonus, pc = prep
    shp = lambda a: a.reshape(n_seq, n_chunks * chunk, C_B)
    pc4 = pc.reshape(n_seq, n_chunks, 1, C_B)
    tok = pl.BlockSpec((1, chunk, C_B), lambda b, c: (b, c, 0))
    state = pl.BlockSpec((1, N_HEADS_B, HEAD_B, HEAD_B), lambda b, c: (b, 0, 0, 0))
    vec = pl.BlockSpec((1, C_B), lambda b, c: (0, 0))
    body = functools.partial(_rwkv_scan_body, chunk=chunk, n_chunks=n_chunks)
    return pl.pallas_call(
        body,
        out_shape=(jax.ShapeDtypeStruct((n_seq, n_chunks * chunk, C_B), y_dtype),
                   jax.ShapeDtypeStruct((n_seq, N_HEADS_B, HEAD_B, HEAD_B), F32)),
        grid=(n_seq, n_chunks),
        in_specs=[tok] * 8 + [pl.BlockSpec((1, 1, 1, C_B), lambda b, c: (b, c, 0, 0)), state, vec, vec],
        out_specs=(tok, state),
        scratch_shapes=[pltpu.VMEM((N_HEADS_B // 2, LANES, LANES), F32)],
        compiler_params=_cp(("parallel", "arbitrary"),
                            blocks=[((chunk, 9 * C_B), F32), ((2 * N_HEADS_B, HEAD_B, LANES), F32)],
                            scratch=[((N_HEADS_B // 2, LANES, LANES), F32)],
                            values=[((N_HEADS_B // 2, 2 * chunk, LANES), F32)] * 24),
        name="rwkv_scan",
    )(shp(at), shp(rt), shp(kh), shp(bh), shp(kb), shp(bb), shp(vv), shp(bonus), pc4, s0, lnw, lnb)


def _merge_body(x_ref, attn_ref, rwo_ref, ga_ref, gb_ref, wo_ref, fw_ref, o_ref, *, final):
    up = lambda ref: ref[...].astype(F32)
    merged = up(ga_ref) * up(attn_ref) + up(gb_ref) * up(rwo_ref)
    xn = x_ref[...] + _dot(merged.astype(BF16), wo_ref[...])
    if final:
        xn = xn * lax.rsqrt(jnp.mean(xn * xn, axis=-1, keepdims=True) + RMS_EPS) * fw_ref[...]
    o_ref[...] = xn


def _merge(x2d, attn, rwo, gate_a, gate_b, wo, fw, tm, final):
    rows, d = x2d.shape
    rowb = lambda i: (i, 0)
    fixed = lambda i: (0, 0)
    return pl.pallas_call(
        functools.partial(_merge_body, final=final),
        out_shape=jax.ShapeDtypeStruct((rows, d), F32),
        grid=(rows // tm,),
        in_specs=[pl.BlockSpec((tm, d), rowb)] * 5 + [pl.BlockSpec((d, d), fixed), pl.BlockSpec((1, d), fixed)],
        out_specs=pl.BlockSpec((tm, d), rowb),
        compiler_params=_cp(("parallel",), blocks=[((tm, 6 * d), F32), ((d, d), BF16)], values=[((tm, d), F32)] * 2),
        name="merge_out",
    )(x2d, attn, rwo, gate_a, gate_b, wo, fw)


def _rope_tables(pos):
    inv = jnp.power(ROPE_THETA, -jnp.arange(HALF, dtype=F32) / HALF)
    ang = pos.astype(F32)[:, None] * inv[None, :]
    return jnp.cos(ang), jnp.sin(ang)


def _row_tables(cos, sin):
    return jnp.tile(cos, (1, LANES // HALF)), jnp.tile(jnp.concatenate([-sin, sin], axis=1), (1, LANES // HEAD_DIM))


def _pad_rows_to(x, rows, at):
    out = jnp.zeros((rows,) + x.shape[1:], x.dtype)
    return lax.dynamic_update_slice_in_dim(out, x, at, axis=0)


def _layer_weights(w_in_l, idx_k_ln_w_l, idx_k_ln_b_l, w_up_l, a_up_l):
    seg, off = {}, 0
    for name, size in _SEG_SIZES:
        seg[name] = w_in_l[:, off:off + size].astype(BF16)
        off += size
    pad = jnp.zeros((D_MODEL, LANES - D_IDX - N_IDX_HEADS), BF16)
    seg["ikw"] = jnp.concatenate([seg["ik"], seg["iw"], pad], axis=1)
    zeros = jnp.zeros((LANES - D_IDX,), F32)
    seg["lnw"] = jnp.concatenate([idx_k_ln_w_l, zeros]).reshape(1, LANES)
    seg["lnb"] = jnp.concatenate([idx_k_ln_b_l, zeros]).reshape(1, LANES)
    seg["wup"] = _pad_rows_to(w_up_l.astype(BF16), LANES, 0)
    seg["aup"] = _pad_rows_to(a_up_l.astype(BF16), LANES, R_W)
    return seg


def _rwkv_branch(p_rw, first_prev, s0, rw_params, lnw, lnb, n_seq, seq_len, chunk, tb, blocks_per_seq, store_dtype):
    prep = _rwkv_prep(p_rw, first_prev, rw_params, tb, chunk, blocks_per_seq, store_dtype)
    y, s_out = _rwkv_scan(prep, s0, lnw, lnb, n_seq, seq_len // chunk, chunk, store_dtype)
    return y.reshape(n_seq * seq_len, C_B), s_out


def kernel(x_prompt, x_sample, cache_k, cache_v, cache_idx_k, state_wkv, state_shift, page_table, norm_w, w_in, idx_k_ln_w, idx_k_ln_b, mu_shift, w0, w_up, a0, a_up, k_k, k_a, r_k, ln_x_w, ln_x_b, w_o, final_norm_w):
    batch, seq, _ = x_prompt.shape
    n_dec, n_new, _ = x_sample.shape
    depth = w_in.shape[0]
    n_pool, page = cache_k.shape[1], cache_k.shape[2]
    n_pages = page_table.shape[1]
    past = n_pages * page
    rows_p, rows_s = batch * seq, n_dec * n_new

    cos_p, sin_p = _rope_tables(jnp.arange(seq))
    cos_s, sin_s = _rope_tables(past + jnp.arange(n_new))
    cos_s, sin_s = jnp.tile(cos_s, (n_dec, 1)), jnp.tile(sin_s, (n_dec, 1))
    cos_pr, sin_pr = _row_tables(jnp.tile(cos_p, (batch, 1)), jnp.tile(sin_p, (batch, 1)))
    cos_sr, sin_sr = _row_tables(cos_s, sin_s)
    cos_pc, sin_pc = cos_p.T, sin_p.T

    xp = x_prompt.reshape(rows_p, D_MODEL)
    xs = x_sample.reshape(rows_s, D_MODEL)
    outs_p = {n: [] for n in ("k", "v", "ik", "wkv", "sh")}
    outs_s = {n: [] for n in ("k", "v", "ik", "wkv", "sh")}
    head_group = jnp.arange(N_HEADS_A) // HEADS_PER_KV
    kv_onehot = head_group[:, None] == jnp.arange(N_KV)[None, :]

    for l in range(depth):
        wl = _layer_weights(w_in[l], idx_k_ln_w[l], idx_k_ln_b[l], w_up[l], a_up[l])
        vec = lambda a: a.reshape(1, -1)
        rw_params = (vec(mu_shift[l]), vec(w0[l]), wl["wup"], vec(a0[l]), wl["aup"], vec(k_k[l]), vec(k_a[l]), vec(r_k[l]))
        lnw, lnb = vec(ln_x_w[l]), vec(ln_x_b[l])
        wo = w_o[l].astype(BF16)
        final = l == depth - 1
        fw = vec(final_norm_w)

        h = _rmsnorm(xp, norm_w[l], BF16, ROW_TILE)
        k_p, k_bf = _proj_rope(h, wl["k"], cos_pr, sin_pr, ROW_TILE, KV_W, "proj_k")
        v_p = _proj_plain(h, wl["v"], ROW_TILE, KV_W, "proj_v")
        ikw = _proj_ikw(h, wl["ikw"], cos_pr, sin_pr, wl["lnw"], wl["lnb"], ROW_TILE)
        ik_p = ikw[:, :D_IDX]
        qt, iqt, vt, iwt = _proj_t(h, wl["q"].T, wl["iq"].T, wl["v"].T, wl["iw"].T, cos_pc, sin_pc, batch, seq, ROW_TILE)
        gate_a = _proj_gate(h, wl["za"], wl["ga"], WIDE_ROW_TILE, GATE_COL_TILE, "proj_gate_a")
        gate_b = _proj_gate(h, wl["zb"], wl["gb"], WIDE_ROW_TILE, GATE_COL_TILE, "proj_gate_b")
        p_rw = _proj_plain(h, wl["rw"], WIDE_ROW_TILE, RW_COL_TILE, "proj_rw")
        attn = _attn_prompt(k_bf.reshape(batch, seq, KV_W), vt, ik_p.astype(BF16).reshape(batch, seq, D_IDX),
                            qt, iqt, iwt, min(TOPK_MAX, seq // 4), ATT_Q_TILE, ATT_KEY_CHUNK)
        rwo, wkv = _rwkv_branch(p_rw, jnp.zeros((batch, 1, N_SHIFT), F32),
                                jnp.zeros((batch, N_HEADS_B, HEAD_B, HEAD_B), F32), rw_params, lnw, lnb,
                                batch, seq, RWKV_CHUNK, RWKV_PREP_ROWS, seq // RWKV_PREP_ROWS, BF16)
        xp = _merge(xp, attn.reshape(rows_p, C_A), rwo, gate_a, gate_b, wo, fw, ROW_TILE, final)
        outs_p["k"].append(k_p.reshape(batch, seq, N_KV, HEAD_DIM))
        outs_p["v"].append(v_p.reshape(batch, seq, N_KV, HEAD_DIM))
        outs_p["ik"].append(ik_p.reshape(batch, seq, D_IDX))
        outs_p["wkv"].append(wkv)
        outs_p["sh"].append(p_rw.reshape(batch, seq, N_SHIFT)[:, -1])

        h = _rmsnorm(xs, norm_w[l], BF16, ROW_TILE)
        w_qkiq = jnp.concatenate([wl["q"], wl["k"], wl["iq"]], axis=1)
        qki, _ = _proj_rope(h, w_qkiq, cos_sr, sin_sr, ROW_TILE, KV_W, "proj_qkiq_s")
        q_s, k_s, iq_s = qki[:, :C_A], qki[:, C_A:C_A + KV_W], qki[:, C_A + KV_W:]
        v_s = _proj_plain(h, wl["v"], ROW_TILE, KV_W, "proj_v_s")
        ikw = _proj_ikw(h, wl["ikw"], cos_sr, sin_sr, wl["lnw"], wl["lnb"], ROW_TILE)
        ik_s, iw_s = ikw[:, :D_IDX], ikw[:, D_IDX:D_IDX + N_IDX_HEADS]
        gate_a = _proj_gate(h, wl["za"], wl["ga"], ROW_TILE, GATE_COL_TILE, "proj_gate_a_s")
        gate_b = _proj_gate(h, wl["zb"], wl["gb"], ROW_TILE, GATE_COL_TILE, "proj_gate_b_s")
        p_rw = _proj_plain(h, wl["rw"], ROW_TILE, RW_COL_TILE, "proj_rw_s")
        q4 = q_s.reshape(n_dec, n_new, N_HEADS_A, HEAD_DIM).transpose(0, 2, 1, 3)
        q2 = jnp.where(kv_onehot[None, :, None, :, None], q4[:, :, :, None, :], 0.0)
        q2 = q2.reshape(n_dec, N_HEADS_A * n_new, KV_W).astype(BF16)
        iq2 = iq_s.reshape(n_dec, n_new, N_IDX_HEADS, D_IDX).transpose(0, 2, 1, 3)
        iq2 = iq2.reshape(n_dec, N_IDX_HEADS * n_new, D_IDX).astype(BF16)
        iwc = iw_s.reshape(n_dec, n_new, N_IDX_HEADS).transpose(0, 2, 1).reshape(n_dec, N_IDX_HEADS * n_new, 1)
        ck_t = jnp.transpose(cache_k[l], (0, 2, 3, 1)).reshape(n_pool, KV_W, page)
        cv_t = jnp.transpose(cache_v[l], (0, 2, 3, 1)).reshape(n_pool, KV_W, page)
        cik_t = jnp.transpose(cache_idx_k[l], (0, 2, 1))
        o2 = _attn_sample(page_table, q2, iq2, iwc, k_s.reshape(n_dec, n_new, KV_W), v_s.reshape(n_dec, n_new, KV_W),
                          ik_s.reshape(n_dec, n_new, D_IDX), ck_t, cv_t, cik_t, min(TOPK_MAX, (past + n_new) // 4),
                          SAMPLE_KEY_CHUNK)
        o5 = o2.reshape(n_dec, N_HEADS_A, n_new, N_KV, HEAD_DIM)
        attn_s = o5[:, jnp.arange(N_HEADS_A), :, head_group, :]
        attn_s = attn_s.transpose(1, 2, 0, 3).reshape(rows_s, C_A)
        rwo, wkv = _rwkv_branch(p_rw, state_shift[l], state_wkv[l], rw_params, lnw, lnb,
                                n_dec, n_new, n_new, SUBLANES * n_new, None, F32)
        xs = _merge(xs, attn_s, rwo, gate_a, gate_b, wo, fw, ROW_TILE, final)
        outs_s["k"].append(k_s.reshape(n_dec, n_new, N_KV, HEAD_DIM))
        outs_s["v"].append(v_s.reshape(n_dec, n_new, N_KV, HEAD_DIM))
        outs_s["ik"].append(ik_s.reshape(n_dec, n_new, D_IDX))
        outs_s["wkv"].append(wkv)
        outs_s["sh"].append(p_rw.reshape(n_dec, n_new, N_SHIFT)[:, -1])

    st = lambda d, n: jnp.stack(d[n])
    return (xp.reshape(batch, seq, D_MODEL), xs.reshape(n_dec, n_new, D_MODEL),
            st(outs_p, "k"), st(outs_p, "v"), st(outs_p, "ik"), st(outs_p, "wkv"), st(outs_p, "sh"),
            st(outs_s, "k"), st(outs_s, "v"), st(outs_s, "ik"), st(outs_s, "wkv"), st(outs_s, "sh"))
```

```python
import functools

import jax
import jax.numpy as jnp
from jax import lax
from jax.experimental import pallas as pl
from jax.experimental.pallas import tpu as pltpu

F32, BF16, I32 = jnp.float32, jnp.bfloat16, jnp.int32

D_MODEL = 1024
N_HEADS_A = 16
HEAD_DIM = 64
N_KV = 4
HEADS_PER_KV = N_HEADS_A // N_KV
C_A = N_HEADS_A * HEAD_DIM
KV_W = N_KV * HEAD_DIM
N_IDX_HEADS = 8
D_IDX = 64
IQ_W = N_IDX_HEADS * D_IDX
TOPK_MAX = 256
ROPE_THETA = 10000.0
HEAD_B = 64
N_HEADS_B = D_MODEL // HEAD_B
C_B = N_HEADS_B * HEAD_B
R_W = 64
R_A = 64
N_SHIFT = 3 * C_B + R_W + R_A
GN_EPS = 64e-5
RMS_EPS = 1e-6
LN_EPS = 1e-6
IW_SCALE = N_IDX_HEADS ** -0.5 * D_IDX ** -0.5
Q_SCALE = HEAD_DIM ** -0.5
LOG2E = 1.4426950408889634
QT_SCALE = Q_SCALE * LOG2E

_SEG_SIZES = (("q", C_A), ("k", KV_W), ("v", KV_W), ("iq", IQ_W), ("ik", D_IDX), ("iw", N_IDX_HEADS),
              ("za", C_A), ("rw", N_SHIFT), ("zb", C_B), ("ga", D_MODEL), ("gb", D_MODEL))

LANES = 128
SUBLANES = 8
HALF = HEAD_DIM // 2

ROW_TILE = 512
WIDE_ROW_TILE = 1024
GATE_COL_TILE = 512
RW_COL_TILE = 640
ATT_Q_TILE = 128
ATT_KEY_CHUNK = 512
SAMPLE_KEY_CHUNK = 1024
RWKV_CHUNK = 64
RWKV_PREP_ROWS = 512
SCAN_SEQS = 4
BISECT_TAIL_ROUNDS = 3
BISECT_BLIND_ROUNDS = 21

INT_MIN = -(2 ** 31)
KEY_NEG_INF = 0x807FFFFF - 2 ** 32
M_INIT = -1e30


V7X_VMEM_BYTES = 64 << 20
VMEM_LIMIT_CAP = V7X_VMEM_BYTES * 7 // 8


def _nbytes(shape, dtype):
    n = jnp.dtype(dtype).itemsize
    for s in shape:
        n *= s
    return n


def _cp(sem, blocks=(), scratch=(), values=()):
    kw = dict(dimension_semantics=sem)
    need = 2 * sum(_nbytes(*b) for b in blocks) + sum(_nbytes(*s) for s in scratch) + sum(_nbytes(*v) for v in values)
    if need:
        kw["vmem_limit_bytes"] = min(VMEM_LIMIT_CAP, max(need, 16 << 20))
    return pltpu.CompilerParams(**kw)


def _dot(a, b):
    return jnp.dot(a, b, preferred_element_type=F32)


def _dot_nt(a, b):
    return lax.dot_general(a, b, (((1,), (1,)), ((), ())), preferred_element_type=F32)


def _dot_tn(a, b):
    return lax.dot_general(a, b, (((0,), (0,)), ((), ())), preferred_element_type=F32)


def _split(x, parts):
    out = []
    for _ in range(parts):
        hi = x.astype(BF16)
        out.append(hi)
        x = x - hi.astype(F32)
    return out


def _dot_exact_rhs(a, b_bf16, parts):
    acc = None
    for t in _split(a, parts):
        d = _dot(t, b_bf16)
        acc = d if acc is None else acc + d
    return acc


def _dot_exact_lhs(a_bf16, b, parts):
    acc = None
    for t in _split(b, parts):
        d = _dot(a_bf16, t)
        acc = d if acc is None else acc + d
    return acc


def _mm3(a, b):
    a_hi, a_lo = _split(a, 2)
    b_hi, b_lo = _split(b, 2)
    return _dot(a_hi, b_hi) + (_dot(a_hi, b_lo) + _dot(a_lo, b_hi))


def _iota(shape, dim):
    return lax.broadcasted_iota(I32, shape, dim)


def _head_ones():
    r = _iota((LANES, LANES), 0) // HEAD_B
    c = _iota((LANES, LANES), 1) // HEAD_B
    return jnp.where(r == c, 1.0, 0.0).astype(BF16)


def _head_sum(x, g):
    return _dot_exact_rhs(x, g, 2)


def _rmsnorm_body(x_ref, w_ref, o_ref):
    x = x_ref[...]
    inv = lax.rsqrt(jnp.mean(x * x, axis=-1, keepdims=True) + RMS_EPS)
    o_ref[...] = (x * inv * w_ref[...]).astype(o_ref.dtype)


def _rmsnorm(x2d, w, out_dtype, tm):
    rows, d = x2d.shape
    return pl.pallas_call(
        _rmsnorm_body,
        out_shape=jax.ShapeDtypeStruct((rows, d), out_dtype),
        grid=(rows // tm,),
        in_specs=[pl.BlockSpec((tm, d), lambda i: (i, 0)), pl.BlockSpec((1, d), lambda i: (0, 0))],
        out_specs=pl.BlockSpec((tm, d), lambda i: (i, 0)),
        compiler_params=_cp(("parallel",)),
        name="rmsnorm",
    )(x2d, w.reshape(1, d))


def _proj_plain_body(h_ref, w_ref, o_ref):
    o_ref[...] = _dot(h_ref[...], w_ref[...]).astype(o_ref.dtype)


def _proj_plain(h, w, tm, tn, name):
    rows, d = h.shape
    n = w.shape[1]
    return pl.pallas_call(
        _proj_plain_body,
        out_shape=jax.ShapeDtypeStruct((rows, n), F32),
        grid=(rows // tm, n // tn),
        in_specs=[pl.BlockSpec((tm, d), lambda i, j: (i, 0)), pl.BlockSpec((d, tn), lambda i, j: (0, j))],
        out_specs=pl.BlockSpec((tm, tn), lambda i, j: (i, j)),
        compiler_params=_cp(("parallel", "arbitrary")),
        name=name,
    )(h, w)


def _rope_rows(x, cos_t, sin_t):
    first = (_iota((x.shape[0], LANES), 1) % HEAD_DIM) < HALF
    outs = []
    for c in range(x.shape[1] // LANES):
        xc = x[:, c * LANES:(c + 1) * LANES]
        partner = jnp.where(first, pltpu.roll(xc, LANES - HALF, 1), pltpu.roll(xc, HALF, 1))
        outs.append(xc * cos_t + partner * sin_t)
    return outs[0] if len(outs) == 1 else jnp.concatenate(outs, axis=1)


def _proj_rope_body(h_ref, w_ref, cos_ref, sin_ref, o_ref, obf_ref):
    y = _rope_rows(_dot(h_ref[...], w_ref[...]), cos_ref[...], sin_ref[...])
    o_ref[...] = y
    obf_ref[...] = y.astype(BF16)


def _proj_rope(h, w, cos_t, sin_t, tm, tn, name):
    rows, d = h.shape
    n = w.shape[1]
    return pl.pallas_call(
        _proj_rope_body,
        out_shape=(jax.ShapeDtypeStruct((rows, n), F32), jax.ShapeDtypeStruct((rows, n), BF16)),
        grid=(rows // tm, n // tn),
        in_specs=[pl.BlockSpec((tm, d), lambda i, j: (i, 0)), pl.BlockSpec((d, tn), lambda i, j: (0, j)),
                  pl.BlockSpec((tm, LANES), lambda i, j: (i, 0)), pl.BlockSpec((tm, LANES), lambda i, j: (i, 0))],
        out_specs=(pl.BlockSpec((tm, tn), lambda i, j: (i, j)), pl.BlockSpec((tm, tn), lambda i, j: (i, j))),
        compiler_params=_cp(("parallel", "arbitrary")),
        name=name,
    )(h, w, cos_t, sin_t)


def _proj_ikw_body(h_ref, w_ref, cos_ref, sin_ref, lnw_ref, lnb_ref, o_ref):
    acc = _dot(h_ref[...], w_ref[...])
    lane = _iota(acc.shape, 1)
    is_ik = lane < D_IDX
    mu = jnp.sum(jnp.where(is_ik, acc, 0.0), axis=-1, keepdims=True) * (1.0 / D_IDX)
    dev = jnp.where(is_ik, acc - mu, 0.0)
    var = jnp.sum(dev * dev, axis=-1, keepdims=True) * (1.0 / D_IDX)
    y = dev * lax.rsqrt(var + LN_EPS) * lnw_ref[...] + lnb_ref[...]
    partner = jnp.where(lane < HALF, pltpu.roll(y, LANES - HALF, 1), pltpu.roll(y, HALF, 1))
    yr = y * cos_ref[...] + partner * sin_ref[...]
    o_ref[...] = jnp.where(is_ik, yr, jnp.where(lane < D_IDX + N_IDX_HEADS, acc * IW_SCALE, 0.0))


def _proj_ikw(h, w, cos_t, sin_t, lnw, lnb, tm):
    rows, d = h.shape
    row = lambda i: (i, 0)
    fixed = lambda i: (0, 0)
    return pl.pallas_call(
        _proj_ikw_body,
        out_shape=jax.ShapeDtypeStruct((rows, LANES), F32),
        grid=(rows // tm,),
        in_specs=[pl.BlockSpec((tm, d), row), pl.BlockSpec((d, LANES), fixed),
                  pl.BlockSpec((tm, LANES), row), pl.BlockSpec((tm, LANES), row),
                  pl.BlockSpec((1, LANES), fixed), pl.BlockSpec((1, LANES), fixed)],
        out_specs=pl.BlockSpec((tm, LANES), row),
        compiler_params=_cp(("parallel",)),
        name="proj_ikw",
    )(h, w, cos_t, sin_t, lnw, lnb)


def _proj_gate_body(h_ref, wz_ref, wg_ref, o_ref):
    h = h_ref[...]
    z = _dot(h, wz_ref[...])
    g = _dot(h, wg_ref[...])
    o_ref[...] = (jax.nn.sigmoid(g) * (z * jax.nn.sigmoid(z))).astype(o_ref.dtype)


def _proj_gate(h, wz, wg, tm, tn, name):
    rows, d = h.shape
    n = wz.shape[1]
    return pl.pallas_call(
        _proj_gate_body,
        out_shape=jax.ShapeDtypeStruct((rows, n), BF16),
        grid=(rows // tm, n // tn),
        in_specs=[pl.BlockSpec((tm, d), lambda i, j: (i, 0)), pl.BlockSpec((d, tn), lambda i, j: (0, j)),
                  pl.BlockSpec((d, tn), lambda i, j: (0, j))],
        out_specs=pl.BlockSpec((tm, tn), lambda i, j: (i, j)),
        compiler_params=_cp(("parallel", "arbitrary")),
        name=name,
    )(h, wz, wg)


def _rope_cols(x, c, s):
    outs = []
    for hh in range(x.shape[0] // HEAD_DIM):
        x0 = x[hh * HEAD_DIM:hh * HEAD_DIM + HALF]
        x1 = x[hh * HEAD_DIM + HALF:(hh + 1) * HEAD_DIM]
        outs.append(x0 * c - x1 * s)
        outs.append(x1 * c + x0 * s)
    return jnp.concatenate(outs, axis=0)


def _proj_t_body(h_ref, wq_ref, wiq_ref, wv_ref, wiw_ref, cos_ref, sin_ref, oq_ref, oiq_ref, ov_ref, oiw_ref):
    h = h_ref[...]
    c = cos_ref[...]
    s = sin_ref[...]
    oq_ref[0] = (_rope_cols(_dot_nt(wq_ref[...], h), c, s) * QT_SCALE).astype(BF16)
    oiq_ref[0] = _rope_cols(_dot_nt(wiq_ref[...], h), c, s).astype(BF16)
    ov_ref[0] = _dot_nt(wv_ref[...], h).astype(BF16)
    oiw_ref[0] = _dot_nt(wiw_ref[...], h) * IW_SCALE


def _proj_t(h, wq_t, wiq_t, wv_t, wiw_t, cos_c, sin_c, batch, seq, tm):
    d = h.shape[1]
    nblk = seq // tm
    fixed = lambda b, i: (0, 0)
    col = lambda b, i: (b, 0, i)
    return pl.pallas_call(
        _proj_t_body,
        out_shape=(jax.ShapeDtypeStruct((batch, C_A, seq), BF16), jax.ShapeDtypeStruct((batch, IQ_W, seq), BF16),
                   jax.ShapeDtypeStruct((batch, KV_W, seq), BF16), jax.ShapeDtypeStruct((batch, N_IDX_HEADS, seq), F32)),
        grid=(batch, nblk),
        in_specs=[pl.BlockSpec((tm, d), lambda b, i: (b * nblk + i, 0)),
                  pl.BlockSpec((C_A, d), fixed), pl.BlockSpec((IQ_W, d), fixed), pl.BlockSpec((KV_W, d), fixed),
                  pl.BlockSpec((N_IDX_HEADS, d), fixed),
                  pl.BlockSpec((HALF, tm), lambda b, i: (0, i)), pl.BlockSpec((HALF, tm), lambda b, i: (0, i))],
        out_specs=(pl.BlockSpec((1, C_A, tm), col), pl.BlockSpec((1, IQ_W, tm), col),
                   pl.BlockSpec((1, KV_W, tm), col), pl.BlockSpec((1, N_IDX_HEADS, tm), col)),
        compiler_params=_cp(("parallel", "parallel"),
                            blocks=[((tm, d), BF16), ((C_A + IQ_W + KV_W + N_IDX_HEADS, d), BF16),
                                    ((C_A + IQ_W + KV_W, tm), BF16)],
                            values=[((C_A + IQ_W + KV_W, tm), F32)] * 2),
        name="proj_transposed",
    )(h, wq_t, wiq_t, wv_t, wiw_t, cos_c, sin_c)


def _key_to_float(key):
    bits = key ^ ((key >> 31) & 0x7FFFFFFF)
    return lax.bitcast_convert_type(bits, F32)


def _float_to_key(x):
    bits = lax.bitcast_convert_type(x, I32)
    return bits ^ ((bits >> 31) & 0x7FFFFFFF)


def _kth_largest_bisect(count_ge, k_lo, k_hi, k):
    def halve(lo, hi):
        active = lo < hi
        mid = (lo >> 1) + (hi >> 1) + (((lo & 1) + (hi & 1) + 1) >> 1)
        (c,) = count_ge([_key_to_float(mid)])
        ge = c >= k
        new_lo = jnp.where(active & ge, mid, lo)
        new_hi = jnp.where(active, jnp.where(ge, jnp.where(c == k, mid, hi), mid - 1), hi)
        return new_lo, new_hi

    def open_rows(lo, hi):
        return jnp.max((lo < hi).astype(I32))

    lo, hi = lax.fori_loop(0, BISECT_BLIND_ROUNDS, lambda i, s: halve(*s), (k_lo, k_hi))

    def tail(state):
        lo, hi = lax.fori_loop(0, BISECT_TAIL_ROUNDS, lambda i, s: halve(*s), (state[0], state[1]))
        return lo, hi, open_rows(lo, hi)

    lo, _, _ = lax.while_loop(lambda s: s[2] > 0, tail, (lo, hi, open_rows(lo, hi)))
    return lo


def _kth_largest_key(count_ge, like, k, rounds):
    zero = jnp.zeros_like(like)
    (c0,) = count_ge([_key_to_float(zero)])
    t = jnp.where(c0 >= k, zero, jnp.full_like(like, INT_MIN))
    top = 31
    for width, n_rounds in rounds:
        def round_fn(i, t, top=top, width=width):
            shift = top - width * (i + 1)
            cands = [t | (jnp.int32(v) << shift) for v in range(1, 2 ** width)]
            counts = count_ge([_key_to_float(c) for c in cands])
            accepted = jnp.zeros_like(t)
            for c in counts:
                accepted = accepted + (c >= k).astype(I32)
            return t | (accepted << shift)

        t = lax.fori_loop(0, n_rounds, round_fn, t)
        top -= width * n_rounds
    assert top == 0
    return t


def _attn_prompt_body(k_ref, vt_ref, ik_ref, qt_ref, iqt_ref, iwt_ref, o_ref,
                      sc_scr, qbd_scr, iqc_scr, m_scr, l_scr, acc_scr, *, tq, tk, seq, topk):
    j = pl.program_id(1)
    nk = ((j + 1) * tq + tk - 1) // tk
    neg_inf = -jnp.inf

    for h in range(N_IDX_HEADS):
        iqc_scr[:, h * tq:(h + 1) * tq] = iqt_ref[0, h * D_IDX:(h + 1) * D_IDX, :]
    qbd_scr[...] = jnp.zeros(qbd_scr.shape, BF16)
    for h in range(N_HEADS_A):
        gp, hh = divmod(h, 2 * HEADS_PER_KV)
        gl = hh // HEADS_PER_KV
        qbd_scr[gp, gl * HEAD_DIM:(gl + 1) * HEAD_DIM, hh * tq:(hh + 1) * tq] = qt_ref[0, h * HEAD_DIM:(h + 1) * HEAD_DIM, :]
    iw = iwt_ref[0]
    t_idx = j * tq + _iota((tk, tq), 1)
    row = _iota((tk, tq), 0)

    def idx_chunk(c, carry):
        off = pl.multiple_of(c * tk, tk)
        dots = _dot(ik_ref[0, pl.ds(off, tk), :], iqc_scr[...])
        acc = jnp.zeros((tk, tq), F32)
        for h in range(N_IDX_HEADS):
            acc = acc + jnp.maximum(dots[:, h * tq:(h + 1) * tq], 0.0) * iw[h:h + 1, :]
        sc_scr[pl.ds(off, tk), :] = jnp.where(off + row <= t_idx, acc, neg_inf)
        return carry

    lax.fori_loop(0, nk, idx_chunk, 0)

    def count(preds):
        def body(c, accs):
            off = pl.multiple_of(c * tk, tk)
            v = sc_scr[pl.ds(off, tk), :]
            s = off + row
            return tuple(a + jnp.sum(p(v, s).astype(I32).reshape(tk // SUBLANES, SUBLANES, tq), axis=0)
                         for a, p in zip(accs, preds))
        accs = lax.fori_loop(0, nk, body, tuple(jnp.zeros((SUBLANES, tq), I32) for _ in preds))
        return [jnp.sum(a, axis=0, keepdims=True) for a in accs]

    def count_ge(cfs):
        return count([lambda v, s, cf=cf: v >= cf for cf in cfs])

    assert tk % topk == 0

    def class_max(c, best):
        off = pl.multiple_of(c * tk, tk)
        v = sc_scr[pl.ds(off, tk), :]
        for part in range(tk // topk):
            best = jnp.maximum(best, v[part * topk:(part + 1) * topk])
        return best

    best = lax.fori_loop(0, nk, class_max, jnp.full((topk, tq), neg_inf, F32))
    k_lo = _float_to_key(jnp.min(best, axis=0, keepdims=True))
    k_hi = _float_to_key(jnp.max(best, axis=0, keepdims=True))
    t_key = _kth_largest_bisect(count_ge, k_lo, k_hi, topk)
    no_thr = t_key <= KEY_NEG_INF
    t_f = jnp.where(no_thr, neg_inf, _key_to_float(t_key))
    n_gt, n_ge = count([lambda v, s: v > t_f, lambda v, s: v >= t_f])
    need = (n_ge > topk) & jnp.logical_not(no_thr)
    room = topk - n_gt
    nbits = (seq - 1).bit_length()

    def x_step(i, x):
        cand = x | (jnp.int32(1) << (nbits - 1 - i))
        (before,) = count([lambda v, s: (v == t_f) & (s < cand)])
        return jnp.where(before < room, cand, x)

    any_need = jnp.max(need.astype(I32))
    x_tie = lax.fori_loop(0, nbits * any_need, x_step, jnp.zeros((1, tq), I32))
    x_lim = jnp.where(no_thr, -1, jnp.where(need, x_tie, seq))

    m_scr[...] = jnp.full(m_scr.shape, M_INIT, F32)
    l_scr[...] = jnp.zeros(l_scr.shape, F32)
    acc_scr[...] = jnp.zeros(acc_scr.shape, F32)
    th = tk // 2
    ones_rows = jnp.ones((2 * SUBLANES, th), BF16)

    def att_chunk(c, carry):
        off = pl.multiple_of(c * tk, tk)
        v = sc_scr[pl.ds(off, tk), :]
        sel = (v > t_f) | ((v == t_f) & (off + row <= x_lim))
        bias = jnp.where(sel, 0.0, neg_inf)
        s8s = [_dot(k_ref[0, pl.ds(off, tk), gp * LANES:(gp + 1) * LANES], qbd_scr[gp]) for gp in range(N_KV // 2)]
        for half in range(2):
            rows = slice(half * th, (half + 1) * th)
            off_h = pl.multiple_of(off + half * th, th)
            bias_h = bias[rows]
            for g in range(N_KV):
                gp, gl = divmod(g, 2)
                ps, alphas = [], []
                for hl in range(HEADS_PER_KV):
                    hh = gl * HEADS_PER_KV + hl
                    h = g * HEADS_PER_KV + hl
                    sh = s8s[gp][rows, hh * tq:(hh + 1) * tq] + bias_h
                    m_old = m_scr[h:h + 1, :]
                    m_new = jnp.maximum(m_old, jnp.max(sh, axis=0, keepdims=True))
                    m_scr[h:h + 1, :] = m_new
                    ps.append(jnp.exp2(sh - m_new).astype(BF16))
                    alphas.append(jnp.exp2(m_old - m_new))
                vals = jnp.concatenate([vt_ref[0, g * HEAD_DIM:(g + 1) * HEAD_DIM, pl.ds(off_h, th)], ones_rows], axis=0)
                pv = _dot(vals, jnp.concatenate(ps, axis=1))
                for hl in range(HEADS_PER_KV):
                    h = g * HEADS_PER_KV + hl
                    hd = slice(h * HEAD_DIM, (h + 1) * HEAD_DIM)
                    cols = slice(hl * tq, (hl + 1) * tq)
                    acc_scr[hd, :] = acc_scr[hd, :] * alphas[hl] + pv[0:HEAD_DIM, cols]
                    l_scr[h:h + 1, :] = l_scr[h:h + 1, :] * alphas[hl] + pv[HEAD_DIM:HEAD_DIM + 1, cols]
        return carry

    lax.fori_loop(0, nk, att_chunk, 0)

    for h in range(N_HEADS_A):
        acc_scr[h * HEAD_DIM:(h + 1) * HEAD_DIM, :] = acc_scr[h * HEAD_DIM:(h + 1) * HEAD_DIM, :] / l_scr[h:h + 1, :]
    o_ref[0] = acc_scr[...].T.astype(o_ref.dtype)


def _attn_prompt(k_bf, vt, ik_bf, qt, iqt, iwt, topk, tq, tk):
    batch, seq, _ = k_bf.shape
    full = lambda b, j: (b, 0, 0)
    col = lambda b, j: (b, 0, j)
    body = functools.partial(_attn_prompt_body, tq=tq, tk=tk, seq=seq, topk=topk)
    return pl.pallas_call(
        body,
        out_shape=jax.ShapeDtypeStruct((batch, seq, C_A), BF16),
        grid=(batch, seq // tq),
        in_specs=[pl.BlockSpec((1, seq, KV_W), full), pl.BlockSpec((1, KV_W, seq), full),
                  pl.BlockSpec((1, seq, D_IDX), full),
                  pl.BlockSpec((1, C_A, tq), col), pl.BlockSpec((1, IQ_W, tq), col),
                  pl.BlockSpec((1, N_IDX_HEADS, tq), col)],
        out_specs=pl.BlockSpec((1, tq, C_A), lambda b, j: (b, j, 0)),
        scratch_shapes=[pltpu.VMEM((seq, tq), F32),
                        pltpu.VMEM((N_KV // 2, LANES, 2 * HEADS_PER_KV * tq), BF16),
                        pltpu.VMEM((D_IDX, N_IDX_HEADS * tq), BF16),
                        pltpu.VMEM((N_HEADS_A, tq), F32), pltpu.VMEM((N_HEADS_A, tq), F32),
                        pltpu.VMEM((C_A, tq), F32)],
        compiler_params=_cp(("parallel", "arbitrary"),
                            blocks=[((seq, 2 * KV_W), BF16), ((seq, LANES), BF16), ((C_A + IQ_W, tq), BF16), ((tq, C_A), F32)],
                            scratch=[((seq, tq), F32), ((2 * LANES, C_A), BF16), ((C_A, tq), F32)],
                            values=[((tk, C_A), F32)] * 3),
        name="attn_prompt",
    )(k_bf, vt, ik_bf, qt, iqt, iwt)


def _attn_sample_body(pt_ref, q2_ref, iq2_ref, iwc_ref, kn_ref, vn_ref, ikn_ref, ck_hbm, cv_hbm, cik_hbm, o_ref,
                      kbuf, vbuf, ikbuf, isc_scr, s_scr, sem, *, n_seq, n_pages, page, n_new, topk, tl):
    b = pl.program_id(0)
    slot = b % 2
    past = n_pages * page
    n_chunks = past // tl
    rows_q = N_HEADS_A * n_new
    rows_i = N_IDX_HEADS * n_new
    neg_inf = -jnp.inf

    def page_copies(seq_i, slot_i, p):
        pg = pt_ref[seq_i, p]
        dst = pl.ds(pl.multiple_of(p * page, page), page)
        return (pltpu.make_async_copy(ck_hbm.at[pg], kbuf.at[slot_i, :, dst], sem.at[slot_i, 0]),
                pltpu.make_async_copy(cv_hbm.at[pg], vbuf.at[slot_i, :, dst], sem.at[slot_i, 1]),
                pltpu.make_async_copy(cik_hbm.at[pg], ikbuf.at[slot_i, :, dst], sem.at[slot_i, 2]))

    def start_seq(seq_i, slot_i):
        def body(p, carry):
            for cp in page_copies(seq_i, slot_i, p):
                cp.start()
            return carry
        lax.fori_loop(0, n_pages, body, 0, unroll=8)

    def wait_seq(seq_i, slot_i):
        def body(p, carry):
            for cp in page_copies(seq_i, slot_i, p):
                cp.wait()
            return carry
        lax.fori_loop(0, n_pages, body, 0, unroll=8)

    @pl.when(b == 0)
    def _():
        start_seq(0, 0)

    @pl.when(b + 1 < n_seq)
    def _():
        start_seq(b + 1, 1 - slot)

    wait_seq(b, slot)

    iq2 = iq2_ref[0]
    iwc = jnp.broadcast_to(iwc_ref[0], (rows_i, tl))

    def head_sum(d):
        return jnp.sum(d.reshape(N_IDX_HEADS, n_new, d.shape[1]), axis=0)

    def idx_chunk(c, carry):
        off = pl.multiple_of(c * tl, tl)
        d = _dot(iq2, ikbuf[slot, :, pl.ds(off, tl)].astype(BF16))
        isc_scr[:, pl.ds(off, tl)] = head_sum(jnp.maximum(d, 0.0) * iwc)
        return carry

    lax.fori_loop(0, n_chunks, idx_chunk, 0)
    pad_rows = LANES - n_new
    ik_new = jnp.concatenate([ikn_ref[0], jnp.zeros((pad_rows, D_IDX), F32)], axis=0).astype(BF16)
    lane_id = _iota((n_new, LANES), 1)
    sc_new = head_sum(jnp.maximum(_dot_nt(iq2, ik_new), 0.0) * iwc[:, :LANES])
    isc_scr[:, past:past + LANES] = jnp.where(lane_id <= _iota((n_new, LANES), 0), sc_new, neg_inf)
    total = past + LANES
    lane_tl = _iota((n_new, tl), 1)

    def wide(x):
        return jnp.broadcast_to(x, (n_new, tl))

    def fit(xw, v):
        return xw if v.shape[1] == tl else xw[:, :v.shape[1]]

    def count(preds):
        def fold(acc, hits):
            for t in range(hits.shape[1] // LANES):
                acc = acc + hits[:, t * LANES:(t + 1) * LANES]
            return acc

        def body(c, accs):
            off = pl.multiple_of(c * tl, tl)
            v = isc_scr[:, pl.ds(off, tl)]
            s = off + lane_tl
            return tuple(fold(a, jnp.where(p(v, s), 1.0, 0.0)) for a, p in zip(accs, preds))

        accs = lax.fori_loop(0, n_chunks, body, tuple(jnp.zeros((n_new, LANES), F32) for _ in preds))
        v_n = isc_scr[:, past:past + LANES]
        s_n = past + lane_id
        return [jnp.sum(a + jnp.where(p(v_n, s_n), 1.0, 0.0), axis=1, keepdims=True) for a, p in zip(accs, preds)]

    def count_ge(cfs):
        return count([lambda v, s, cw=wide(cf): v >= fit(cw, v) for cf in cfs])

    kf = float(topk)
    t_key = _kth_largest_key(count_ge, jnp.zeros((n_new, 1), I32), kf, [(1, 1), (2, 15)])
    no_thr = t_key <= KEY_NEG_INF
    t_f = jnp.where(no_thr, neg_inf, _key_to_float(t_key))
    t_w = wide(t_f)
    n_gt, n_ge = count([lambda v, s: v > fit(t_w, v), lambda v, s: v >= fit(t_w, v)])
    need = (n_ge > kf) & jnp.logical_not(no_thr)
    room = kf - n_gt
    nbits = (total - 1).bit_length()

    def x_step(i, x):
        c_w = wide(x | (jnp.int32(1) << (nbits - 1 - i)))
        (before,) = count([lambda v, s: (v == fit(t_w, v)) & (s < fit(c_w, v))])
        return jnp.where(before < room, x | (jnp.int32(1) << (nbits - 1 - i)), x)

    any_need = jnp.max(need.astype(I32))
    x_tie = lax.fori_loop(0, nbits * any_need, x_step, jnp.zeros((n_new, 1), I32))
    x_w = wide(jnp.where(no_thr, -1, jnp.where(need, x_tie, total)))

    def sel_bias(v, s):
        sel = (v > fit(t_w, v)) | ((v == fit(t_w, v)) & (s <= fit(x_w, v)))
        b8 = jnp.where(sel, 0.0, neg_inf)
        return jnp.concatenate([b8] * N_HEADS_A, axis=0)

    q2 = q2_ref[0]

    def score_chunk(c, m):
        off = pl.multiple_of(c * tl, tl)
        s = _dot(q2, kbuf[slot, :, pl.ds(off, tl)].astype(BF16)) * Q_SCALE
        s = s + sel_bias(isc_scr[:, pl.ds(off, tl)], off + lane_tl)
        s_scr[:, pl.ds(off, tl)] = s
        return jnp.maximum(m, jnp.max(s, axis=1, keepdims=True))

    m = lax.fori_loop(0, n_chunks, score_chunk, jnp.full((rows_q, 1), M_INIT, F32))
    k_new = jnp.concatenate([kn_ref[0], jnp.zeros((pad_rows, KV_W), F32)], axis=0).astype(BF16)
    v_new = jnp.concatenate([vn_ref[0], jnp.zeros((pad_rows, KV_W), F32)], axis=0).astype(BF16)
    s_n = _dot_nt(q2, k_new) * Q_SCALE + sel_bias(isc_scr[:, past:past + LANES], past + lane_id)
    m = jnp.maximum(m, jnp.max(s_n, axis=1, keepdims=True))
    p_n = jnp.exp(s_n - m)
    l0 = jnp.sum(p_n, axis=1, keepdims=True)
    o0 = _dot(p_n.astype(BF16), v_new)
    m_w = jnp.broadcast_to(m, (rows_q, tl))

    def pv_chunk(c, carry):
        l, o = carry
        off = pl.multiple_of(c * tl, tl)
        p = jnp.exp(s_scr[:, pl.ds(off, tl)] - m_w)
        l = l + jnp.sum(p, axis=1, keepdims=True)
        o = o + _dot_nt(p.astype(BF16), vbuf[slot, :, pl.ds(off, tl)].astype(BF16))
        return l, o

    l, o = lax.fori_loop(0, n_chunks, pv_chunk, (l0, o0))
    o_ref[0] = o / l


def _attn_sample(page_table, q2, iq2, iwc, k_new, v_new, ik_new, cache_k, cache_v, cache_ik, topk, tl):
    n_seq, n_pages = page_table.shape
    _, _, page = cache_k.shape
    n_new = k_new.shape[1]
    past = n_pages * page
    rows_q = N_HEADS_A * n_new
    rows_i = N_IDX_HEADS * n_new
    per_seq = lambda b, pt: (b, 0, 0)
    body = functools.partial(_attn_sample_body, n_seq=n_seq, n_pages=n_pages, page=page, n_new=n_new, topk=topk, tl=tl)
    grid_spec = pltpu.PrefetchScalarGridSpec(
        num_scalar_prefetch=1,
        grid=(n_seq,),
        in_specs=[pl.BlockSpec((1, rows_q, KV_W), per_seq), pl.BlockSpec((1, rows_i, D_IDX), per_seq),
                  pl.BlockSpec((1, rows_i, 1), per_seq),
                  pl.BlockSpec((1, n_new, KV_W), per_seq), pl.BlockSpec((1, n_new, KV_W), per_seq),
                  pl.BlockSpec((1, n_new, D_IDX), per_seq),
                  pl.BlockSpec(memory_space=pl.ANY), pl.BlockSpec(memory_space=pl.ANY),
                  pl.BlockSpec(memory_space=pl.ANY)],
        out_specs=pl.BlockSpec((1, rows_q, KV_W), per_seq),
        scratch_shapes=[pltpu.VMEM((2, KV_W, past), F32), pltpu.VMEM((2, KV_W, past), F32),
                        pltpu.VMEM((2, D_IDX, past), F32),
                        pltpu.VMEM((n_new, past + LANES), F32), pltpu.VMEM((rows_q, past), F32),
                        pltpu.SemaphoreType.DMA((2, 3))],
    )
    return pl.pallas_call(
        body,
        out_shape=jax.ShapeDtypeStruct((n_seq, rows_q, KV_W), F32),
        grid_spec=grid_spec,
        compiler_params=_cp(("arbitrary",),
                            blocks=[((2 * rows_q, KV_W), F32)],
                            scratch=[((2, 2 * KV_W + D_IDX, past), F32), ((n_new + rows_q, past + LANES), F32)],
                            values=[((KV_W, tl), F32), ((rows_q, tl), F32)] * 2),
        name="attn_sample",
    )(page_table, q2, iq2, iwc, k_new, v_new, ik_new, cache_k, cache_v, cache_ik)


def _rwkv_prep_body(p_ref, halo_ref, first_ref, mu_ref, w0_ref, wup_ref, a0_ref, aup_ref, kk_ref, ka_ref, rk_ref,
                    at_ref, rt_ref, kh_ref, bh_ref, kb_ref, bb_ref, v_ref, bonus_ref, pc_ref,
                    *, tb, chunk, blocks_per_seq):
    p = p_ref[...]
    row = _iota((tb, 1), 0)
    rolled = pltpu.roll(p, 1, 0)
    if blocks_per_seq is None:
        n_seq = tb // chunk
        first = first_ref[...]
        expanded = jnp.broadcast_to(first[:, None, :], (n_seq, chunk, N_SHIFT)).reshape(tb, N_SHIFT)
        prev = jnp.where(row % chunk == 0, expanded, rolled)
    else:
        i = pl.program_id(0)
        starts_seq = (i % blocks_per_seq) == 0
        row0 = jnp.where(starts_seq, first_ref[0], halo_ref[SUBLANES - 1:SUBLANES, :])
        prev = jnp.where(row == 0, row0, rolled)
    xs = p + (prev - p) * mu_ref[...]
    r = xs[:, 0:C_B]
    k = xs[:, C_B:2 * C_B]
    v = xs[:, 2 * C_B:3 * C_B]
    wa = xs[:, 3 * C_B:N_SHIFT]
    lane = _iota((tb, LANES), 1)
    t = jnp.where(lane < R_W, jnp.tanh(wa), wa).astype(BF16)
    log_w = -jax.nn.softplus(-(w0_ref[...] + _dot(t, wup_ref[...]))) - 0.5
    ld = -jnp.exp(log_w)
    a_sig = jax.nn.sigmoid(a0_ref[...] + _dot(t, aup_ref[...]))
    g = _head_ones()
    kk = k * kk_ref[...]
    k2 = k * (1.0 + (a_sig - 1.0) * ka_ref[...])
    rk = r * k2 * rk_ref[...]
    tri = jnp.where(_iota((chunk, chunk), 1) <= _iota((chunk, chunk), 0), 1.0, 0.0).astype(BF16)
    for ch in range(tb // chunk):
        rows = slice(ch * chunk, (ch + 1) * chunk)
        ld_c = ld[rows]
        cum = _dot_exact_lhs(tri, ld_c, 3)
        tot = cum[chunk - 1:chunk]
        rem = tot - cum
        pc_ref[0, ch:ch + 1, :] = jnp.exp(tot)
        e_cum = jnp.exp(cum)
        e_inv = jnp.exp(-cum)
        e_rem = jnp.exp(rem)
        e_prev = jnp.exp(cum - ld_c)
        for c in range(C_B // LANES):
            sl = slice(c * LANES, (c + 1) * LANES)
            kk_c = kk[rows, sl]
            nrm = jnp.maximum(jnp.sqrt(_head_sum(kk_c * kk_c, g)), 1e-12)
            kkn = kk_c / nrm
            bv = kkn * a_sig[rows, sl]
            at_ref[rows, sl] = (kkn * e_prev[:, sl]).astype(at_ref.dtype)
            rt_ref[rows, sl] = (r[rows, sl] * e_cum[:, sl]).astype(rt_ref.dtype)
            kh_ref[rows, sl] = (k2[rows, sl] * e_inv[:, sl]).astype(kh_ref.dtype)
            bh_ref[rows, sl] = (bv * e_inv[:, sl]).astype(bh_ref.dtype)
            kb_ref[rows, sl] = (k2[rows, sl] * e_rem[:, sl]).astype(kb_ref.dtype)
            bb_ref[rows, sl] = (bv * e_rem[:, sl]).astype(bb_ref.dtype)
            v_ref[rows, sl] = v[rows, sl].astype(v_ref.dtype)
            bonus_ref[rows, sl] = _head_sum(rk[rows, sl], g) * v[rows, sl]


def _rwkv_prep(p_rw, first_prev, params, tb, chunk, blocks_per_seq, store_dtype):
    rows = p_rw.shape[0]
    mu, w0, wup, a0, aup, kk, ka, rk = params
    nblk = rows // tb
    n_chunks = tb // chunk
    rowb = lambda i: (i, 0)
    fixed = lambda i: (0, 0)
    if blocks_per_seq is None:
        halo_spec = pl.BlockSpec((SUBLANES, N_SHIFT), fixed)
        first_spec = pl.BlockSpec((n_chunks, N_SHIFT), rowb)
    else:
        halo_spec = pl.BlockSpec((SUBLANES, N_SHIFT), lambda i: (jnp.maximum(i * (tb // SUBLANES) - 1, 0), 0))
        first_spec = pl.BlockSpec((1, 1, N_SHIFT), lambda i: (i // blocks_per_seq, 0, 0))
    vec = lambda n: pl.BlockSpec((1, n), fixed)
    wide = jax.ShapeDtypeStruct((rows, C_B), store_dtype)
    body = functools.partial(_rwkv_prep_body, tb=tb, chunk=chunk, blocks_per_seq=blocks_per_seq)
    return pl.pallas_call(
        body,
        out_shape=(wide,) * 7 + (jax.ShapeDtypeStruct((rows, C_B), F32),
                                 jax.ShapeDtypeStruct((nblk, n_chunks, C_B), F32)),
        grid=(nblk,),
        in_specs=[pl.BlockSpec((tb, N_SHIFT), rowb), halo_spec, first_spec, vec(N_SHIFT), vec(C_B),
                  pl.BlockSpec((LANES, C_B), fixed), vec(C_B), pl.BlockSpec((LANES, C_B), fixed),
                  vec(C_B), vec(C_B), vec(C_B)],
        out_specs=(pl.BlockSpec((tb, C_B), rowb),) * 8 + (pl.BlockSpec((1, n_chunks, C_B), lambda i: (i, 0, 0)),),
        compiler_params=_cp(("parallel",),
                            blocks=[((tb, N_SHIFT), F32), ((tb, 7 * C_B), store_dtype), ((tb, C_B), F32), ((2 * LANES, C_B), BF16)],
                            values=[((tb, N_SHIFT), F32)] * 4),
        name="rwkv_prep",
    )(p_rw, p_rw, first_prev, mu, w0, wup, a0, aup, kk, ka, rk)


def _rwkv_scan_body(at_ref, rt_ref, kh_ref, bh_ref, kb_ref, bb_ref, v_ref, bonus_ref, pc_ref, s0_ref, lnw_ref, lnb_ref,
                    y_ref, sout_ref, s_scr, *, chunk, n_chunks, n_par):
    c = pl.program_id(1)
    two = 2 * chunk
    n_pairs = N_HEADS_B // 2
    items = [(s, pr) for s in range(n_par) for pr in range(n_pairs)]

    @pl.when(c == 0)
    def _():
        s_scr[...] = jnp.zeros(s_scr.shape, F32)
        for i, (s, pr) in enumerate(items):
            s_scr[i, 0:HEAD_B, 0:HEAD_B] = s0_ref[s, 2 * pr]
            s_scr[i, HEAD_B:LANES, HEAD_B:LANES] = s0_ref[s, 2 * pr + 1]

    ri = _iota((two, two), 0)
    ci = _iota((two, two), 1)
    same = (ri // chunk) == (ci // chunk)
    strict = same & (ci < ri)
    incl = same & (ci <= ri)
    stack_mask = (_iota((two, LANES), 0) // chunk) == (_iota((two, LANES), 1) // HEAD_B)
    eye = jnp.where(ri == ci, 1.0, 0.0)
    g = _head_ones()
    n_factors = (chunk - 1).bit_length()

    def stack(ref, item):
        s, pr = item
        x = ref[s, :, pr * LANES:(pr + 1) * LANES].astype(BF16)
        return jnp.where(stack_mask, jnp.concatenate([x, x], axis=0), jnp.zeros((two, LANES), BF16))

    def mm(x, y):
        return _dot(x.astype(BF16), y.astype(BF16))

    pairs = range(len(items))
    sls = [slice(pr * LANES, (pr + 1) * LANES) for _, pr in items]
    a_s = [stack(at_ref, it) for it in items]
    b_s = [stack(bh_ref, it) for it in items]
    m_ab = [jnp.where(strict, _dot_nt(a_s[p], b_s[p]), 0.0) for p in pairs]
    pw = [-m for m in m_ab]
    t_inv = [eye + x for x in pw]
    for _ in range(n_factors - 1):
        pw = [mm(x, x) for x in pw]
        t_inv = [t + mm(t, x) for t, x in zip(t_inv, pw)]
    k_s = [stack(kh_ref, it) for it in items]
    r_s = [stack(rt_ref, it) for it in items]
    v_s = [stack(v_ref, it) for it in items]
    m_ak = [jnp.where(strict, _dot_nt(a_s[p], k_s[p]), 0.0).astype(BF16) for p in pairs]
    m_rb = [jnp.where(incl, _dot_nt(r_s[p], b_s[p]), 0.0).astype(BF16) for p in pairs]
    m_rk = [jnp.where(incl, _dot_nt(r_s[p], k_s[p]), 0.0).astype(BF16) for p in pairs]
    s_old = [s_scr[p] for p in pairs]
    s_bf = [s.astype(BF16) for s in s_old]
    w = [_dot_nt(a_s[p], s_bf[p]) + _dot(m_ak[p], v_s[p]) for p in pairs]
    u_bf = [(-mm(t_inv[p], w[p])).astype(BF16) for p in pairs]
    y2 = [_dot_nt(r_s[p], s_bf[p]) + _dot(m_rk[p], v_s[p]) + _dot(m_rb[p], u_bf[p]) for p in pairs]
    kb_s = [stack(kb_ref, it) for it in items]
    bb_s = [stack(bb_ref, it) for it in items]
    for p, (s, _) in enumerate(items):
        s_scr[p] = s_old[p] * pc_ref[s, 0, :, sls[p]] + _dot_tn(v_s[p], kb_s[p]) + _dot_tn(u_bf[p], bb_s[p])
    ys = [y[0:chunk] + y[chunk:two] for y in y2]
    means = [mm(y, g) * (1.0 / HEAD_B) for y in ys]
    devs = [y - m for y, m in zip(ys, means)]
    vars_ = [mm(d * d, g) * (1.0 / HEAD_B) for d in devs]
    for p, (s, _) in enumerate(items):
        sl = sls[p]
        y = devs[p] * lax.rsqrt(vars_[p] + GN_EPS) * lnw_ref[:, sl] + lnb_ref[:, sl] + bonus_ref[s, :, sl]
        y_ref[s, :, sl] = y.astype(y_ref.dtype)

    @pl.when(c == n_chunks - 1)
    def _():
        for i, (s, pr) in enumerate(items):
            sout_ref[s, 2 * pr] = s_scr[i, 0:HEAD_B, 0:HEAD_B]
            sout_ref[s, 2 * pr + 1] = s_scr[i, HEAD_B:LANES, HEAD_B:LANES]


def _rwkv_scan(prep, s0, lnw, lnb, n_seq, n_chunks, chunk, y_dtype):
    at, rt, kh, bh, kb, bb, vv, bonus, pc = prep
    n_par = min(SCAN_SEQS, n_seq)
    n_items = n_par * (N_HEADS_B // 2)
    shp = lambda a: a.reshape(n_seq, n_chunks * chunk, C_B)
    pc4 = pc.reshape(n_seq, n_chunks, 1, C_B)
    tok = pl.BlockSpec((n_par, chunk, C_B), lambda b, c: (b, c, 0))
    state = pl.BlockSpec((n_par, N_HEADS_B, HEAD_B, HEAD_B), lambda b, c: (b, 0, 0, 0))
    vec = pl.BlockSpec((1, C_B), lambda b, c: (0, 0))
    body = functools.partial(_rwkv_scan_body, chunk=chunk, n_chunks=n_chunks, n_par=n_par)
    return pl.pallas_call(
        body,
        out_shape=(jax.ShapeDtypeStruct((n_seq, n_chunks * chunk, C_B), y_dtype),
                   jax.ShapeDtypeStruct((n_seq, N_HEADS_B, HEAD_B, HEAD_B), F32)),
        grid=(n_seq // n_par, n_chunks),
        in_specs=[tok] * 8 + [pl.BlockSpec((n_par, 1, 1, C_B), lambda b, c: (b, c, 0, 0)), state, vec, vec],
        out_specs=(tok, state),
        scratch_shapes=[pltpu.VMEM((n_items, LANES, LANES), F32)],
        compiler_params=_cp(("parallel", "arbitrary"),
                            blocks=[((n_par * chunk, 9 * C_B), F32), ((n_par * 2 * N_HEADS_B, HEAD_B, LANES), F32)],
                            scratch=[((n_items, LANES, LANES), F32)],
                            values=[((n_items, 2 * chunk, LANES), F32)] * 24),
        name="rwkv_scan",
    )(shp(at), shp(rt), shp(kh), shp(bh), shp(kb), shp(bb), shp(vv), shp(bonus), pc4, s0, lnw, lnb)


def _merge_body(x_ref, attn_ref, rwo_ref, ga_ref, gb_ref, wo_ref, fw_ref, o_ref, *, final):
    up = lambda ref: ref[...].astype(F32)
    merged = up(ga_ref) * up(attn_ref) + up(gb_ref) * up(rwo_ref)
    xn = x_ref[...] + _dot(merged.astype(BF16), wo_ref[...])
    if final:
        xn = xn * lax.rsqrt(jnp.mean(xn * xn, axis=-1, keepdims=True) + RMS_EPS) * fw_ref[...]
    o_ref[...] = xn


def _merge(x2d, attn, rwo, gate_a, gate_b, wo, fw, tm, final):
    rows, d = x2d.shape
    rowb = lambda i: (i, 0)
    fixed = lambda i: (0, 0)
    return pl.pallas_call(
        functools.partial(_merge_body, final=final),
        out_shape=jax.ShapeDtypeStruct((rows, d), F32),
        grid=(rows // tm,),
        in_specs=[pl.BlockSpec((tm, d), rowb)] * 5 + [pl.BlockSpec((d, d), fixed), pl.BlockSpec((1, d), fixed)],
        out_specs=pl.BlockSpec((tm, d), rowb),
        compiler_params=_cp(("parallel",), blocks=[((tm, 6 * d), F32), ((d, d), BF16)], values=[((tm, d), F32)] * 2),
        name="merge_out",
    )(x2d, attn, rwo, gate_a, gate_b, wo, fw)


def _rope_tables(pos):
    inv = jnp.power(ROPE_THETA, -jnp.arange(HALF, dtype=F32) / HALF)
    ang = pos.astype(F32)[:, None] * inv[None, :]
    return jnp.cos(ang), jnp.sin(ang)


def _row_tables(cos, sin):
    return jnp.tile(cos, (1, LANES // HALF)), jnp.tile(jnp.concatenate([-sin, sin], axis=1), (1, LANES // HEAD_DIM))


def _pad_rows_to(x, rows, at):
    out = jnp.zeros((rows,) + x.shape[1:], x.dtype)
    return lax.dynamic_update_slice_in_dim(out, x, at, axis=0)


def _layer_weights(w_in_l, idx_k_ln_w_l, idx_k_ln_b_l, w_up_l, a_up_l):
    seg, off = {}, 0
    for name, size in _SEG_SIZES:
        seg[name] = w_in_l[:, off:off + size].astype(BF16)
        off += size
    pad = jnp.zeros((D_MODEL, LANES - D_IDX - N_IDX_HEADS), BF16)
    seg["ikw"] = jnp.concatenate([seg["ik"], seg["iw"], pad], axis=1)
    zeros = jnp.zeros((LANES - D_IDX,), F32)
    seg["lnw"] = jnp.concatenate([idx_k_ln_w_l, zeros]).reshape(1, LANES)
    seg["lnb"] = jnp.concatenate([idx_k_ln_b_l, zeros]).reshape(1, LANES)
    seg["wup"] = _pad_rows_to(w_up_l.astype(BF16), LANES, 0)
    seg["aup"] = _pad_rows_to(a_up_l.astype(BF16), LANES, R_W)
    return seg


def _rwkv_branch(p_rw, first_prev, s0, rw_params, lnw, lnb, n_seq, seq_len, chunk, tb, blocks_per_seq, store_dtype):
    prep = _rwkv_prep(p_rw, first_prev, rw_params, tb, chunk, blocks_per_seq, store_dtype)
    y, s_out = _rwkv_scan(prep, s0, lnw, lnb, n_seq, seq_len // chunk, chunk, store_dtype)
    return y.reshape(n_seq * seq_len, C_B), s_out


def kernel(x_prompt, x_sample, cache_k, cache_v, cache_idx_k, state_wkv, state_shift, page_table, norm_w, w_in, idx_k_ln_w, idx_k_ln_b, mu_shift, w0, w_up, a0, a_up, k_k, k_a, r_k, ln_x_w, ln_x_b, w_o, final_norm_w):
    batch, seq, _ = x_prompt.shape
    n_dec, n_new, _ = x_sample.shape
    depth = w_in.shape[0]
    n_pool, page = cache_k.shape[1], cache_k.shape[2]
    n_pages = page_table.shape[1]
    past = n_pages * page
    rows_p, rows_s = batch * seq, n_dec * n_new

    cos_p, sin_p = _rope_tables(jnp.arange(seq))
    cos_s, sin_s = _rope_tables(past + jnp.arange(n_new))
    cos_s, sin_s = jnp.tile(cos_s, (n_dec, 1)), jnp.tile(sin_s, (n_dec, 1))
    cos_pr, sin_pr = _row_tables(jnp.tile(cos_p, (batch, 1)), jnp.tile(sin_p, (batch, 1)))
    cos_sr, sin_sr = _row_tables(cos_s, sin_s)
    cos_pc, sin_pc = cos_p.T, sin_p.T

    xp = x_prompt.reshape(rows_p, D_MODEL)
    xs = x_sample.reshape(rows_s, D_MODEL)
    outs_p = {n: [] for n in ("k", "v", "ik", "wkv", "sh")}
    outs_s = {n: [] for n in ("k", "v", "ik", "wkv", "sh")}
    head_group = jnp.arange(N_HEADS_A) // HEADS_PER_KV
    kv_onehot = head_group[:, None] == jnp.arange(N_KV)[None, :]

    for l in range(depth):
        wl = _layer_weights(w_in[l], idx_k_ln_w[l], idx_k_ln_b[l], w_up[l], a_up[l])
        vec = lambda a: a.reshape(1, -1)
        rw_params = (vec(mu_shift[l]), vec(w0[l]), wl["wup"], vec(a0[l]), wl["aup"], vec(k_k[l]), vec(k_a[l]), vec(r_k[l]))
        lnw, lnb = vec(ln_x_w[l]), vec(ln_x_b[l])
        wo = w_o[l].astype(BF16)
        final = l == depth - 1
        fw = vec(final_norm_w)

        h = _rmsnorm(xp, norm_w[l], BF16, ROW_TILE)
        k_p, k_bf = _proj_rope(h, wl["k"], cos_pr, sin_pr, ROW_TILE, KV_W, "proj_k")
        v_p = _proj_plain(h, wl["v"], ROW_TILE, KV_W, "proj_v")
        ikw = _proj_ikw(h, wl["ikw"], cos_pr, sin_pr, wl["lnw"], wl["lnb"], ROW_TILE)
        ik_p = ikw[:, :D_IDX]
        qt, iqt, vt, iwt = _proj_t(h, wl["q"].T, wl["iq"].T, wl["v"].T, wl["iw"].T, cos_pc, sin_pc, batch, seq, ROW_TILE)
        gate_a = _proj_gate(h, wl["za"], wl["ga"], WIDE_ROW_TILE, GATE_COL_TILE, "proj_gate_a")
        gate_b = _proj_gate(h, wl["zb"], wl["gb"], WIDE_ROW_TILE, GATE_COL_TILE, "proj_gate_b")
        p_rw = _proj_plain(h, wl["rw"], WIDE_ROW_TILE, RW_COL_TILE, "proj_rw")
        attn = _attn_prompt(k_bf.reshape(batch, seq, KV_W), vt, ik_p.astype(BF16).reshape(batch, seq, D_IDX),
                            qt, iqt, iwt, min(TOPK_MAX, seq // 4), ATT_Q_TILE, ATT_KEY_CHUNK)
        rwo, wkv = _rwkv_branch(p_rw, jnp.zeros((batch, 1, N_SHIFT), F32),
                                jnp.zeros((batch, N_HEADS_B, HEAD_B, HEAD_B), F32), rw_params, lnw, lnb,
                                batch, seq, RWKV_CHUNK, RWKV_PREP_ROWS, seq // RWKV_PREP_ROWS, BF16)
        xp = _merge(xp, attn.reshape(rows_p, C_A), rwo, gate_a, gate_b, wo, fw, ROW_TILE, final)
        outs_p["k"].append(k_p.reshape(batch, seq, N_KV, HEAD_DIM))
        outs_p["v"].append(v_p.reshape(batch, seq, N_KV, HEAD_DIM))
        outs_p["ik"].append(ik_p.reshape(batch, seq, D_IDX))
        outs_p["wkv"].append(wkv)
        outs_p["sh"].append(p_rw.reshape(batch, seq, N_SHIFT)[:, -1])

        h = _rmsnorm(xs, norm_w[l], BF16, ROW_TILE)
        w_qkiq = jnp.concatenate([wl["q"], wl["k"], wl["iq"]], axis=1)
        qki, _ = _proj_rope(h, w_qkiq, cos_sr, sin_sr, ROW_TILE, KV_W, "proj_qkiq_s")
        q_s, k_s, iq_s = qki[:, :C_A], qki[:, C_A:C_A + KV_W], qki[:, C_A + KV_W:]
        v_s = _proj_plain(h, wl["v"], ROW_TILE, KV_W, "proj_v_s")
        ikw = _proj_ikw(h, wl["ikw"], cos_sr, sin_sr, wl["lnw"], wl["lnb"], ROW_TILE)
        ik_s, iw_s = ikw[:, :D_IDX], ikw[:, D_IDX:D_IDX + N_IDX_HEADS]
        gate_a = _proj_gate(h, wl["za"], wl["ga"], ROW_TILE, GATE_COL_TILE, "proj_gate_a_s")
        gate_b = _proj_gate(h, wl["zb"], wl["gb"], ROW_TILE, GATE_COL_TILE, "proj_gate_b_s")
        p_rw = _proj_plain(h, wl["rw"], ROW_TILE, RW_COL_TILE, "proj_rw_s")
        q4 = q_s.reshape(n_dec, n_new, N_HEADS_A, HEAD_DIM).transpose(0, 2, 1, 3)
        q2 = jnp.where(kv_onehot[None, :, None, :, None], q4[:, :, :, None, :], 0.0)
        q2 = q2.reshape(n_dec, N_HEADS_A * n_new, KV_W).astype(BF16)
        iq2 = iq_s.reshape(n_dec, n_new, N_IDX_HEADS, D_IDX).transpose(0, 2, 1, 3)
        iq2 = iq2.reshape(n_dec, N_IDX_HEADS * n_new, D_IDX).astype(BF16)
        iwc = iw_s.reshape(n_dec, n_new, N_IDX_HEADS).transpose(0, 2, 1).reshape(n_dec, N_IDX_HEADS * n_new, 1)
        ck_t = jnp.transpose(cache_k[l], (0, 2, 3, 1)).reshape(n_pool, KV_W, page)
        cv_t = jnp.transpose(cache_v[l], (0, 2, 3, 1)).reshape(n_pool, KV_W, page)
        cik_t = jnp.transpose(cache_idx_k[l], (0, 2, 1))
        o2 = _attn_sample(page_table, q2, iq2, iwc, k_s.reshape(n_dec, n_new, KV_W), v_s.reshape(n_dec, n_new, KV_W),
                          ik_s.reshape(n_dec, n_new, D_IDX), ck_t, cv_t, cik_t, min(TOPK_MAX, (past + n_new) // 4),
                          SAMPLE_KEY_CHUNK)
        o5 = o2.reshape(n_dec, N_HEADS_A, n_new, N_KV, HEAD_DIM)
        attn_s = o5[:, jnp.arange(N_HEADS_A), :, head_group, :]
        attn_s = attn_s.transpose(1, 2, 0, 3).reshape(rows_s, C_A)
        rwo, wkv = _rwkv_branch(p_rw, state_shift[l], state_wkv[l], rw_params, lnw, lnb,
                                n_dec, n_new, n_new, SUBLANES * n_new, None, F32)
        xs = _merge(xs, attn_s, rwo, gate_a, gate_b, wo, fw, ROW_TILE, final)
        outs_s["k"].append(k_s.reshape(n_dec, n_new, N_KV, HEAD_DIM))
        outs_s["v"].append(v_s.reshape(n_dec, n_new, N_KV, HEAD_DIM))
        outs_s["ik"].append(ik_s.reshape(n_dec, n_new, D_IDX))
        outs_s["wkv"].append(wkv)
        outs_s["sh"].append(p_rw.reshape(n_dec, n_new, N_SHIFT)[:, -1])

    st = lambda d, n: jnp.stack(d[n])
    return (xp.reshape(batch, seq, D_MODEL), xs.reshape(n_dec, n_new, D_MODEL),
            st(outs_p, "k"), st(outs_p, "v"), st(outs_p, "ik"), st(outs_p, "wkv"), st(outs_p, "sh"),
            st(outs_s, "k"), st(outs_s, "v"), st(outs_s, "ik"), st(outs_s, "wkv"), st(outs_s, "sh"))
```

```python
import functools

import jax
import jax.numpy as jnp
from jax import lax
from jax.experimental import pallas as pl
from jax.experimental.pallas import tpu as pltpu

F32, BF16, I32 = jnp.float32, jnp.bfloat16, jnp.int32

D_MODEL = 1024
N_HEADS_A = 16
HEAD_DIM = 64
N_KV = 4
HEADS_PER_KV = N_HEADS_A // N_KV
C_A = N_HEADS_A * HEAD_DIM
KV_W = N_KV * HEAD_DIM
N_IDX_HEADS = 8
D_IDX = 64
IQ_W = N_IDX_HEADS * D_IDX
TOPK_MAX = 256
ROPE_THETA = 10000.0
HEAD_B = 64
N_HEADS_B = D_MODEL // HEAD_B
C_B = N_HEADS_B * HEAD_B
R_W = 64
R_A = 64
N_SHIFT = 3 * C_B + R_W + R_A
GN_EPS = 64e-5
RMS_EPS = 1e-6
LN_EPS = 1e-6
IW_SCALE = N_IDX_HEADS ** -0.5 * D_IDX ** -0.5
Q_SCALE = HEAD_DIM ** -0.5
LOG2E = 1.4426950408889634
QT_SCALE = Q_SCALE * LOG2E

_SEG_SIZES = (("q", C_A), ("k", KV_W), ("v", KV_W), ("iq", IQ_W), ("ik", D_IDX), ("iw", N_IDX_HEADS),
              ("za", C_A), ("rw", N_SHIFT), ("zb", C_B), ("ga", D_MODEL), ("gb", D_MODEL))

LANES = 128
SUBLANES = 8
HALF = HEAD_DIM // 2

ROW_TILE = 512
WIDE_ROW_TILE = 1024
GATE_COL_TILE = 512
RW_COL_TILE = 640
ATT_Q_TILE = 128
ATT_KEY_CHUNK = 512
SAMPLE_KEY_CHUNK = 4096
RWKV_CHUNK = 64
RWKV_PREP_ROWS = 512
SCAN_SEQS = 4
BISECT_TAIL_ROUNDS = 3
BISECT_BLIND_ROUNDS = 21

INT_MIN = -(2 ** 31)
KEY_NEG_INF = 0x807FFFFF - 2 ** 32
M_INIT = -1e30


V7X_VMEM_BYTES = 64 << 20
VMEM_LIMIT_CAP = V7X_VMEM_BYTES * 7 // 8


def _nbytes(shape, dtype):
    n = jnp.dtype(dtype).itemsize
    for s in shape:
        n *= s
    return n


def _cp(sem, blocks=(), scratch=(), values=()):
    kw = dict(dimension_semantics=sem)
    need = 2 * sum(_nbytes(*b) for b in blocks) + sum(_nbytes(*s) for s in scratch) + sum(_nbytes(*v) for v in values)
    if need:
        kw["vmem_limit_bytes"] = min(VMEM_LIMIT_CAP, max(need, 16 << 20))
    return pltpu.CompilerParams(**kw)


def _dot(a, b):
    return jnp.dot(a, b, preferred_element_type=F32)


def _dot_nt(a, b):
    return lax.dot_general(a, b, (((1,), (1,)), ((), ())), preferred_element_type=F32)


def _dot_tn(a, b):
    return lax.dot_general(a, b, (((0,), (0,)), ((), ())), preferred_element_type=F32)


def _split(x, parts):
    out = []
    for _ in range(parts):
        hi = x.astype(BF16)
        out.append(hi)
        x = x - hi.astype(F32)
    return out


def _dot_exact_rhs(a, b_bf16, parts):
    acc = None
    for t in _split(a, parts):
        d = _dot(t, b_bf16)
        acc = d if acc is None else acc + d
    return acc


def _dot_exact_lhs(a_bf16, b, parts):
    acc = None
    for t in _split(b, parts):
        d = _dot(a_bf16, t)
        acc = d if acc is None else acc + d
    return acc


def _mm3(a, b):
    a_hi, a_lo = _split(a, 2)
    b_hi, b_lo = _split(b, 2)
    return _dot(a_hi, b_hi) + (_dot(a_hi, b_lo) + _dot(a_lo, b_hi))


def _iota(shape, dim):
    return lax.broadcasted_iota(I32, shape, dim)


def _head_ones():
    r = _iota((LANES, LANES), 0) // HEAD_B
    c = _iota((LANES, LANES), 1) // HEAD_B
    return jnp.where(r == c, 1.0, 0.0).astype(BF16)


def _head_sum(x, g):
    return _dot_exact_rhs(x, g, 2)


def _rmsnorm_body(x_ref, w_ref, o_ref):
    x = x_ref[...]
    inv = lax.rsqrt(jnp.mean(x * x, axis=-1, keepdims=True) + RMS_EPS)
    o_ref[...] = (x * inv * w_ref[...]).astype(o_ref.dtype)


def _rmsnorm(x2d, w, out_dtype, tm):
    rows, d = x2d.shape
    return pl.pallas_call(
        _rmsnorm_body,
        out_shape=jax.ShapeDtypeStruct((rows, d), out_dtype),
        grid=(rows // tm,),
        in_specs=[pl.BlockSpec((tm, d), lambda i: (i, 0)), pl.BlockSpec((1, d), lambda i: (0, 0))],
        out_specs=pl.BlockSpec((tm, d), lambda i: (i, 0)),
        compiler_params=_cp(("parallel",)),
        name="rmsnorm",
    )(x2d, w.reshape(1, d))


def _proj_plain_body(h_ref, w_ref, o_ref):
    o_ref[...] = _dot(h_ref[...], w_ref[...]).astype(o_ref.dtype)


def _proj_plain(h, w, tm, tn, name):
    rows, d = h.shape
    n = w.shape[1]
    return pl.pallas_call(
        _proj_plain_body,
        out_shape=jax.ShapeDtypeStruct((rows, n), F32),
        grid=(rows // tm, n // tn),
        in_specs=[pl.BlockSpec((tm, d), lambda i, j: (i, 0)), pl.BlockSpec((d, tn), lambda i, j: (0, j))],
        out_specs=pl.BlockSpec((tm, tn), lambda i, j: (i, j)),
        compiler_params=_cp(("parallel", "arbitrary")),
        name=name,
    )(h, w)


def _rope_rows(x, cos_t, sin_t):
    first = (_iota((x.shape[0], LANES), 1) % HEAD_DIM) < HALF
    outs = []
    for c in range(x.shape[1] // LANES):
        xc = x[:, c * LANES:(c + 1) * LANES]
        partner = jnp.where(first, pltpu.roll(xc, LANES - HALF, 1), pltpu.roll(xc, HALF, 1))
        outs.append(xc * cos_t + partner * sin_t)
    return outs[0] if len(outs) == 1 else jnp.concatenate(outs, axis=1)


def _proj_rope_body(h_ref, w_ref, cos_ref, sin_ref, o_ref, obf_ref):
    y = _rope_rows(_dot(h_ref[...], w_ref[...]), cos_ref[...], sin_ref[...])
    o_ref[...] = y
    obf_ref[...] = y.astype(BF16)


def _proj_rope(h, w, cos_t, sin_t, tm, tn, name):
    rows, d = h.shape
    n = w.shape[1]
    return pl.pallas_call(
        _proj_rope_body,
        out_shape=(jax.ShapeDtypeStruct((rows, n), F32), jax.ShapeDtypeStruct((rows, n), BF16)),
        grid=(rows // tm, n // tn),
        in_specs=[pl.BlockSpec((tm, d), lambda i, j: (i, 0)), pl.BlockSpec((d, tn), lambda i, j: (0, j)),
                  pl.BlockSpec((tm, LANES), lambda i, j: (i, 0)), pl.BlockSpec((tm, LANES), lambda i, j: (i, 0))],
        out_specs=(pl.BlockSpec((tm, tn), lambda i, j: (i, j)), pl.BlockSpec((tm, tn), lambda i, j: (i, j))),
        compiler_params=_cp(("parallel", "arbitrary")),
        name=name,
    )(h, w, cos_t, sin_t)


def _proj_ikw_body(h_ref, w_ref, cos_ref, sin_ref, lnw_ref, lnb_ref, o_ref):
    acc = _dot(h_ref[...], w_ref[...])
    lane = _iota(acc.shape, 1)
    is_ik = lane < D_IDX
    mu = jnp.sum(jnp.where(is_ik, acc, 0.0), axis=-1, keepdims=True) * (1.0 / D_IDX)
    dev = jnp.where(is_ik, acc - mu, 0.0)
    var = jnp.sum(dev * dev, axis=-1, keepdims=True) * (1.0 / D_IDX)
    y = dev * lax.rsqrt(var + LN_EPS) * lnw_ref[...] + lnb_ref[...]
    partner = jnp.where(lane < HALF, pltpu.roll(y, LANES - HALF, 1), pltpu.roll(y, HALF, 1))
    yr = y * cos_ref[...] + partner * sin_ref[...]
    o_ref[...] = jnp.where(is_ik, yr, jnp.where(lane < D_IDX + N_IDX_HEADS, acc * IW_SCALE, 0.0))


def _proj_ikw(h, w, cos_t, sin_t, lnw, lnb, tm):
    rows, d = h.shape
    row = lambda i: (i, 0)
    fixed = lambda i: (0, 0)
    return pl.pallas_call(
        _proj_ikw_body,
        out_shape=jax.ShapeDtypeStruct((rows, LANES), F32),
        grid=(rows // tm,),
        in_specs=[pl.BlockSpec((tm, d), row), pl.BlockSpec((d, LANES), fixed),
                  pl.BlockSpec((tm, LANES), row), pl.BlockSpec((tm, LANES), row),
                  pl.BlockSpec((1, LANES), fixed), pl.BlockSpec((1, LANES), fixed)],
        out_specs=pl.BlockSpec((tm, LANES), row),
        compiler_params=_cp(("parallel",)),
        name="proj_ikw",
    )(h, w, cos_t, sin_t, lnw, lnb)


def _proj_gate_body(h_ref, wz_ref, wg_ref, o_ref):
    h = h_ref[...]
    z = _dot(h, wz_ref[...])
    g = _dot(h, wg_ref[...])
    o_ref[...] = (jax.nn.sigmoid(g) * (z * jax.nn.sigmoid(z))).astype(o_ref.dtype)


def _proj_gate(h, wz, wg, tm, tn, name):
    rows, d = h.shape
    n = wz.shape[1]
    return pl.pallas_call(
        _proj_gate_body,
        out_shape=jax.ShapeDtypeStruct((rows, n), BF16),
        grid=(rows // tm, n // tn),
        in_specs=[pl.BlockSpec((tm, d), lambda i, j: (i, 0)), pl.BlockSpec((d, tn), lambda i, j: (0, j)),
                  pl.BlockSpec((d, tn), lambda i, j: (0, j))],
        out_specs=pl.BlockSpec((tm, tn), lambda i, j: (i, j)),
        compiler_params=_cp(("parallel", "arbitrary")),
        name=name,
    )(h, wz, wg)


def _rope_cols(x, c, s):
    outs = []
    for hh in range(x.shape[0] // HEAD_DIM):
        x0 = x[hh * HEAD_DIM:hh * HEAD_DIM + HALF]
        x1 = x[hh * HEAD_DIM + HALF:(hh + 1) * HEAD_DIM]
        outs.append(x0 * c - x1 * s)
        outs.append(x1 * c + x0 * s)
    return jnp.concatenate(outs, axis=0)


def _proj_t_body(h_ref, wq_ref, wiq_ref, wv_ref, wiw_ref, cos_ref, sin_ref, oq_ref, oiq_ref, ov_ref, oiw_ref):
    h = h_ref[...]
    c = cos_ref[...]
    s = sin_ref[...]
    oq_ref[0] = (_rope_cols(_dot_nt(wq_ref[...], h), c, s) * QT_SCALE).astype(BF16)
    oiq_ref[0] = _rope_cols(_dot_nt(wiq_ref[...], h), c, s).astype(BF16)
    ov_ref[0] = _dot_nt(wv_ref[...], h).astype(BF16)
    oiw_ref[0] = _dot_nt(wiw_ref[...], h) * IW_SCALE


def _proj_t(h, wq_t, wiq_t, wv_t, wiw_t, cos_c, sin_c, batch, seq, tm):
    d = h.shape[1]
    nblk = seq // tm
    fixed = lambda b, i: (0, 0)
    col = lambda b, i: (b, 0, i)
    return pl.pallas_call(
        _proj_t_body,
        out_shape=(jax.ShapeDtypeStruct((batch, C_A, seq), BF16), jax.ShapeDtypeStruct((batch, IQ_W, seq), BF16),
                   jax.ShapeDtypeStruct((batch, KV_W, seq), BF16), jax.ShapeDtypeStruct((batch, N_IDX_HEADS, seq), F32)),
        grid=(batch, nblk),
        in_specs=[pl.BlockSpec((tm, d), lambda b, i: (b * nblk + i, 0)),
                  pl.BlockSpec((C_A, d), fixed), pl.BlockSpec((IQ_W, d), fixed), pl.BlockSpec((KV_W, d), fixed),
                  pl.BlockSpec((N_IDX_HEADS, d), fixed),
                  pl.BlockSpec((HALF, tm), lambda b, i: (0, i)), pl.BlockSpec((HALF, tm), lambda b, i: (0, i))],
        out_specs=(pl.BlockSpec((1, C_A, tm), col), pl.BlockSpec((1, IQ_W, tm), col),
                   pl.BlockSpec((1, KV_W, tm), col), pl.BlockSpec((1, N_IDX_HEADS, tm), col)),
        compiler_params=_cp(("parallel", "parallel"),
                            blocks=[((tm, d), BF16), ((C_A + IQ_W + KV_W + N_IDX_HEADS, d), BF16),
                                    ((C_A + IQ_W + KV_W, tm), BF16)],
                            values=[((C_A + IQ_W + KV_W, tm), F32)] * 2),
        name="proj_transposed",
    )(h, wq_t, wiq_t, wv_t, wiw_t, cos_c, sin_c)


def _key_to_float(key):
    bits = key ^ ((key >> 31) & 0x7FFFFFFF)
    return lax.bitcast_convert_type(bits, F32)


def _float_to_key(x):
    bits = lax.bitcast_convert_type(x, I32)
    return bits ^ ((bits >> 31) & 0x7FFFFFFF)


def _kth_largest_bisect(count_ge, k_lo, k_hi, k):
    def halve(lo, hi):
        active = lo < hi
        mid = (lo >> 1) + (hi >> 1) + (((lo & 1) + (hi & 1) + 1) >> 1)
        (c,) = count_ge([_key_to_float(mid)])
        ge = c >= k
        new_lo = jnp.where(active & ge, mid, lo)
        new_hi = jnp.where(active, jnp.where(ge, jnp.where(c == k, mid, hi), mid - 1), hi)
        return new_lo, new_hi

    def open_rows(lo, hi):
        return jnp.max((lo < hi).astype(I32))

    lo, hi = lax.fori_loop(0, BISECT_BLIND_ROUNDS, lambda i, s: halve(*s), (k_lo, k_hi))

    def tail(state):
        lo, hi = lax.fori_loop(0, BISECT_TAIL_ROUNDS, lambda i, s: halve(*s), (state[0], state[1]))
        return lo, hi, open_rows(lo, hi)

    lo, _, _ = lax.while_loop(lambda s: s[2] > 0, tail, (lo, hi, open_rows(lo, hi)))
    return lo


def _kth_largest_key(count_ge, like, k, rounds):
    zero = jnp.zeros_like(like)
    (c0,) = count_ge([_key_to_float(zero)])
    t = jnp.where(c0 >= k, zero, jnp.full_like(like, INT_MIN))
    top = 31
    for width, n_rounds in rounds:
        def round_fn(i, t, top=top, width=width):
            shift = top - width * (i + 1)
            cands = [t | (jnp.int32(v) << shift) for v in range(1, 2 ** width)]
            counts = count_ge([_key_to_float(c) for c in cands])
            accepted = jnp.zeros_like(t)
            for c in counts:
                accepted = accepted + (c >= k).astype(I32)
            return t | (accepted << shift)

        t = lax.fori_loop(0, n_rounds, round_fn, t)
        top -= width * n_rounds
    assert top == 0
    return t


def _attn_prompt_body(k_ref, vt_ref, ik_ref, qt_ref, iqt_ref, iwt_ref, o_ref,
                      sc_scr, qbd_scr, iqc_scr, m_scr, l_scr, acc_scr, *, tq, tk, seq, topk):
    j = pl.program_id(1)
    nk = ((j + 1) * tq + tk - 1) // tk
    neg_inf = -jnp.inf

    for h in range(N_IDX_HEADS):
        iqc_scr[:, h * tq:(h + 1) * tq] = iqt_ref[0, h * D_IDX:(h + 1) * D_IDX, :]
    qbd_scr[...] = jnp.zeros(qbd_scr.shape, BF16)
    for h in range(N_HEADS_A):
        gp, hh = divmod(h, 2 * HEADS_PER_KV)
        gl = hh // HEADS_PER_KV
        qbd_scr[gp, gl * HEAD_DIM:(gl + 1) * HEAD_DIM, hh * tq:(hh + 1) * tq] = qt_ref[0, h * HEAD_DIM:(h + 1) * HEAD_DIM, :]
    iw = iwt_ref[0]
    t_idx = j * tq + _iota((tk, tq), 1)
    row = _iota((tk, tq), 0)

    def idx_chunk(c, carry):
        off = pl.multiple_of(c * tk, tk)
        dots = _dot(ik_ref[0, pl.ds(off, tk), :], iqc_scr[...])
        acc = jnp.zeros((tk, tq), F32)
        for h in range(N_IDX_HEADS):
            acc = acc + jnp.maximum(dots[:, h * tq:(h + 1) * tq], 0.0) * iw[h:h + 1, :]
        sc_scr[pl.ds(off, tk), :] = jnp.where(off + row <= t_idx, acc, neg_inf)
        return carry

    lax.fori_loop(0, nk, idx_chunk, 0)

    def count(preds):
        def body(c, accs):
            off = pl.multiple_of(c * tk, tk)
            v = sc_scr[pl.ds(off, tk), :]
            s = off + row
            return tuple(a + jnp.sum(p(v, s).astype(I32).reshape(tk // SUBLANES, SUBLANES, tq), axis=0)
                         for a, p in zip(accs, preds))
        accs = lax.fori_loop(0, nk, body, tuple(jnp.zeros((SUBLANES, tq), I32) for _ in preds))
        return [jnp.sum(a, axis=0, keepdims=True) for a in accs]

    def count_ge(cfs):
        return count([lambda v, s, cf=cf: v >= cf for cf in cfs])

    assert tk % topk == 0

    def class_max(c, best):
        off = pl.multiple_of(c * tk, tk)
        v = sc_scr[pl.ds(off, tk), :]
        for part in range(tk // topk):
            best = jnp.maximum(best, v[part * topk:(part + 1) * topk])
        return best

    best = lax.fori_loop(0, nk, class_max, jnp.full((topk, tq), neg_inf, F32))
    k_lo = _float_to_key(jnp.min(best, axis=0, keepdims=True))
    k_hi = _float_to_key(jnp.max(best, axis=0, keepdims=True))
    t_key = _kth_largest_bisect(count_ge, k_lo, k_hi, topk)
    no_thr = t_key <= KEY_NEG_INF
    t_f = jnp.where(no_thr, neg_inf, _key_to_float(t_key))
    n_gt, n_ge = count([lambda v, s: v > t_f, lambda v, s: v >= t_f])
    need = (n_ge > topk) & jnp.logical_not(no_thr)
    room = topk - n_gt
    nbits = (seq - 1).bit_length()

    def x_step(i, x):
        cand = x | (jnp.int32(1) << (nbits - 1 - i))
        (before,) = count([lambda v, s: (v == t_f) & (s < cand)])
        return jnp.where(before < room, cand, x)

    any_need = jnp.max(need.astype(I32))
    x_tie = lax.fori_loop(0, nbits * any_need, x_step, jnp.zeros((1, tq), I32))
    x_lim = jnp.where(no_thr, -1, jnp.where(need, x_tie, seq))

    m_scr[...] = jnp.full(m_scr.shape, M_INIT, F32)
    l_scr[...] = jnp.zeros(l_scr.shape, F32)
    acc_scr[...] = jnp.zeros(acc_scr.shape, F32)
    th = tk // 2
    ones_rows = jnp.ones((2 * SUBLANES, th), BF16)

    def att_chunk(c, carry):
        off = pl.multiple_of(c * tk, tk)
        v = sc_scr[pl.ds(off, tk), :]
        sel = (v > t_f) | ((v == t_f) & (off + row <= x_lim))
        bias = jnp.where(sel, 0.0, neg_inf)
        s8s = [_dot(k_ref[0, pl.ds(off, tk), gp * LANES:(gp + 1) * LANES], qbd_scr[gp]) for gp in range(N_KV // 2)]
        for half in range(2):
            rows = slice(half * th, (half + 1) * th)
            off_h = pl.multiple_of(off + half * th, th)
            bias_h = bias[rows]
            for g in range(N_KV):
                gp, gl = divmod(g, 2)
                ps, alphas = [], []
                for hl in range(HEADS_PER_KV):
                    hh = gl * HEADS_PER_KV + hl
                    h = g * HEADS_PER_KV + hl
                    sh = s8s[gp][rows, hh * tq:(hh + 1) * tq] + bias_h
                    m_old = m_scr[h:h + 1, :]
                    m_new = jnp.maximum(m_old, jnp.max(sh, axis=0, keepdims=True))
                    m_scr[h:h + 1, :] = m_new
                    ps.append(jnp.exp2(sh - m_new).astype(BF16))
                    alphas.append(jnp.exp2(m_old - m_new))
                vals = jnp.concatenate([vt_ref[0, g * HEAD_DIM:(g + 1) * HEAD_DIM, pl.ds(off_h, th)], ones_rows], axis=0)
                pv = _dot(vals, jnp.concatenate(ps, axis=1))
                for hl in range(HEADS_PER_KV):
                    h = g * HEADS_PER_KV + hl
                    hd = slice(h * HEAD_DIM, (h + 1) * HEAD_DIM)
                    cols = slice(hl * tq, (hl + 1) * tq)
                    acc_scr[hd, :] = acc_scr[hd, :] * alphas[hl] + pv[0:HEAD_DIM, cols]
                    l_scr[h:h + 1, :] = l_scr[h:h + 1, :] * alphas[hl] + pv[HEAD_DIM:HEAD_DIM + 1, cols]
        return carry

    lax.fori_loop(0, nk, att_chunk, 0)

    for h in range(N_HEADS_A):
        acc_scr[h * HEAD_DIM:(h + 1) * HEAD_DIM, :] = acc_scr[h * HEAD_DIM:(h + 1) * HEAD_DIM, :] / l_scr[h:h + 1, :]
    o_ref[0] = acc_scr[...].T.astype(o_ref.dtype)


def _attn_prompt(k_bf, vt, ik_bf, qt, iqt, iwt, topk, tq, tk):
    batch, seq, _ = k_bf.shape
    full = lambda b, j: (b, 0, 0)
    col = lambda b, j: (b, 0, j)
    body = functools.partial(_attn_prompt_body, tq=tq, tk=tk, seq=seq, topk=topk)
    return pl.pallas_call(
        body,
        out_shape=jax.ShapeDtypeStruct((batch, seq, C_A), BF16),
        grid=(batch, seq // tq),
        in_specs=[pl.BlockSpec((1, seq, KV_W), full), pl.BlockSpec((1, KV_W, seq), full),
                  pl.BlockSpec((1, seq, D_IDX), full),
                  pl.BlockSpec((1, C_A, tq), col), pl.BlockSpec((1, IQ_W, tq), col),
                  pl.BlockSpec((1, N_IDX_HEADS, tq), col)],
        out_specs=pl.BlockSpec((1, tq, C_A), lambda b, j: (b, j, 0)),
        scratch_shapes=[pltpu.VMEM((seq, tq), F32),
                        pltpu.VMEM((N_KV // 2, LANES, 2 * HEADS_PER_KV * tq), BF16),
                        pltpu.VMEM((D_IDX, N_IDX_HEADS * tq), BF16),
                        pltpu.VMEM((N_HEADS_A, tq), F32), pltpu.VMEM((N_HEADS_A, tq), F32),
                        pltpu.VMEM((C_A, tq), F32)],
        compiler_params=_cp(("parallel", "arbitrary"),
                            blocks=[((seq, 2 * KV_W), BF16), ((seq, LANES), BF16), ((C_A + IQ_W, tq), BF16), ((tq, C_A), F32)],
                            scratch=[((seq, tq), F32), ((2 * LANES, C_A), BF16), ((C_A, tq), F32)],
                            values=[((tk, C_A), F32)] * 3),
        name="attn_prompt",
    )(k_bf, vt, ik_bf, qt, iqt, iwt)


def _attn_sample_body(pt_ref, q2_ref, iq2_ref, iwc_ref, kn_ref, vn_ref, ikn_ref, ck_hbm, cv_hbm, cik_hbm, o_ref,
                      kbuf, vbuf, ikbuf, isc_scr, s_scr, sem, *, n_seq, n_pages, page, n_new, topk, tl):
    b = pl.program_id(0)
    slot = b % 2
    past = n_pages * page
    n_chunks = past // tl
    rows_q = N_HEADS_A * n_new
    rows_i = N_IDX_HEADS * n_new
    neg_inf = -jnp.inf

    def page_copies(seq_i, slot_i, p):
        pg = pt_ref[seq_i, p]
        dst = pl.ds(pl.multiple_of(p * page, page), page)
        return (pltpu.make_async_copy(ck_hbm.at[pg], kbuf.at[slot_i, :, dst], sem.at[slot_i, 0]),
                pltpu.make_async_copy(cv_hbm.at[pg], vbuf.at[slot_i, :, dst], sem.at[slot_i, 1]),
                pltpu.make_async_copy(cik_hbm.at[pg], ikbuf.at[slot_i, :, dst], sem.at[slot_i, 2]))

    def start_seq(seq_i, slot_i):
        def body(p, carry):
            for cp in page_copies(seq_i, slot_i, p):
                cp.start()
            return carry
        lax.fori_loop(0, n_pages, body, 0, unroll=8)

    def wait_seq(seq_i, slot_i):
        def body(p, carry):
            for cp in page_copies(seq_i, slot_i, p):
                cp.wait()
            return carry
        lax.fori_loop(0, n_pages, body, 0, unroll=8)

    @pl.when(b == 0)
    def _():
        start_seq(0, 0)

    @pl.when(b + 1 < n_seq)
    def _():
        start_seq(b + 1, 1 - slot)

    wait_seq(b, slot)

    iq2 = iq2_ref[0]
    iwc = jnp.broadcast_to(iwc_ref[0], (rows_i, tl))

    def head_sum(d):
        return jnp.sum(d.reshape(N_IDX_HEADS, n_new, d.shape[1]), axis=0)

    def idx_chunk(c, carry):
        off = pl.multiple_of(c * tl, tl)
        d = _dot(iq2, ikbuf[slot, :, pl.ds(off, tl)].astype(BF16))
        isc_scr[:, pl.ds(off, tl)] = head_sum(jnp.maximum(d, 0.0) * iwc)
        return carry

    lax.fori_loop(0, n_chunks, idx_chunk, 0)
    pad_rows = LANES - n_new
    ik_new = jnp.concatenate([ikn_ref[0], jnp.zeros((pad_rows, D_IDX), F32)], axis=0).astype(BF16)
    lane_id = _iota((n_new, LANES), 1)
    sc_new = head_sum(jnp.maximum(_dot_nt(iq2, ik_new), 0.0) * iwc[:, :LANES])
    isc_scr[:, past:past + LANES] = jnp.where(lane_id <= _iota((n_new, LANES), 0), sc_new, neg_inf)
    total = past + LANES
    lane_tl = _iota((n_new, tl), 1)

    def wide(x):
        return jnp.broadcast_to(x, (n_new, tl))

    def fit(xw, v):
        return xw if v.shape[1] == tl else xw[:, :v.shape[1]]

    def count(preds):
        def fold(acc, hits):
            for t in range(hits.shape[1] // LANES):
                acc = acc + hits[:, t * LANES:(t + 1) * LANES]
            return acc

        def body(c, accs):
            off = pl.multiple_of(c * tl, tl)
            v = isc_scr[:, pl.ds(off, tl)]
            s = off + lane_tl
            return tuple(fold(a, jnp.where(p(v, s), 1.0, 0.0)) for a, p in zip(accs, preds))

        accs = lax.fori_loop(0, n_chunks, body, tuple(jnp.zeros((n_new, LANES), F32) for _ in preds))
        v_n = isc_scr[:, past:past + LANES]
        s_n = past + lane_id
        return [jnp.sum(a + jnp.where(p(v_n, s_n), 1.0, 0.0), axis=1, keepdims=True) for a, p in zip(accs, preds)]

    def count_ge(cfs):
        return count([lambda v, s, cw=wide(cf): v >= fit(cw, v) for cf in cfs])

    kf = float(topk)
    t_key = _kth_largest_key(count_ge, jnp.zeros((n_new, 1), I32), kf, [(1, 1), (2, 15)])
    no_thr = t_key <= KEY_NEG_INF
    t_f = jnp.where(no_thr, neg_inf, _key_to_float(t_key))
    t_w = wide(t_f)
    n_gt, n_ge = count([lambda v, s: v > fit(t_w, v), lambda v, s: v >= fit(t_w, v)])
    need = (n_ge > kf) & jnp.logical_not(no_thr)
    room = kf - n_gt
    nbits = (total - 1).bit_length()

    def x_step(i, x):
        c_w = wide(x | (jnp.int32(1) << (nbits - 1 - i)))
        (before,) = count([lambda v, s: (v == fit(t_w, v)) & (s < fit(c_w, v))])
        return jnp.where(before < room, x | (jnp.int32(1) << (nbits - 1 - i)), x)

    any_need = jnp.max(need.astype(I32))
    x_tie = lax.fori_loop(0, nbits * any_need, x_step, jnp.zeros((n_new, 1), I32))
    x_w = wide(jnp.where(no_thr, -1, jnp.where(need, x_tie, total)))

    def sel_bias(v, s):
        sel = (v > fit(t_w, v)) | ((v == fit(t_w, v)) & (s <= fit(x_w, v)))
        b8 = jnp.where(sel, 0.0, neg_inf)
        return jnp.concatenate([b8] * N_HEADS_A, axis=0)

    q2 = q2_ref[0]

    def score_chunk(c, m):
        off = pl.multiple_of(c * tl, tl)
        s = _dot(q2, kbuf[slot, :, pl.ds(off, tl)].astype(BF16)) * Q_SCALE
        s = s + sel_bias(isc_scr[:, pl.ds(off, tl)], off + lane_tl)
        s_scr[:, pl.ds(off, tl)] = s
        return jnp.maximum(m, jnp.max(s, axis=1, keepdims=True))

    m = lax.fori_loop(0, n_chunks, score_chunk, jnp.full((rows_q, 1), M_INIT, F32))
    k_new = jnp.concatenate([kn_ref[0], jnp.zeros((pad_rows, KV_W), F32)], axis=0).astype(BF16)
    v_new = jnp.concatenate([vn_ref[0], jnp.zeros((pad_rows, KV_W), F32)], axis=0).astype(BF16)
    s_n = _dot_nt(q2, k_new) * Q_SCALE + sel_bias(isc_scr[:, past:past + LANES], past + lane_id)
    m = jnp.maximum(m, jnp.max(s_n, axis=1, keepdims=True))
    p_n = jnp.exp(s_n - m)
    l0 = jnp.sum(p_n, axis=1, keepdims=True)
    o0 = _dot(p_n.astype(BF16), v_new)
    m_w = jnp.broadcast_to(m, (rows_q, tl))

    def pv_chunk(c, carry):
        l, o = carry
        off = pl.multiple_of(c * tl, tl)
        p = jnp.exp(s_scr[:, pl.ds(off, tl)] - m_w)
        l = l + jnp.sum(p, axis=1, keepdims=True)
        o = o + _dot_nt(p.astype(BF16), vbuf[slot, :, pl.ds(off, tl)].astype(BF16))
        return l, o

    l, o = lax.fori_loop(0, n_chunks, pv_chunk, (l0, o0))
    o_ref[0] = o / l


def _attn_sample(page_table, q2, iq2, iwc, k_new, v_new, ik_new, cache_k, cache_v, cache_ik, topk, tl):
    n_seq, n_pages = page_table.shape
    _, _, page = cache_k.shape
    n_new = k_new.shape[1]
    past = n_pages * page
    rows_q = N_HEADS_A * n_new
    rows_i = N_IDX_HEADS * n_new
    per_seq = lambda b, pt: (b, 0, 0)
    body = functools.partial(_attn_sample_body, n_seq=n_seq, n_pages=n_pages, page=page, n_new=n_new, topk=topk, tl=tl)
    grid_spec = pltpu.PrefetchScalarGridSpec(
        num_scalar_prefetch=1,
        grid=(n_seq,),
        in_specs=[pl.BlockSpec((1, rows_q, KV_W), per_seq), pl.BlockSpec((1, rows_i, D_IDX), per_seq),
                  pl.BlockSpec((1, rows_i, 1), per_seq),
                  pl.BlockSpec((1, n_new, KV_W), per_seq), pl.BlockSpec((1, n_new, KV_W), per_seq),
                  pl.BlockSpec((1, n_new, D_IDX), per_seq),
                  pl.BlockSpec(memory_space=pl.ANY), pl.BlockSpec(memory_space=pl.ANY),
                  pl.BlockSpec(memory_space=pl.ANY)],
        out_specs=pl.BlockSpec((1, rows_q, KV_W), per_seq),
        scratch_shapes=[pltpu.VMEM((2, KV_W, past), F32), pltpu.VMEM((2, KV_W, past), F32),
                        pltpu.VMEM((2, D_IDX, past), F32),
                        pltpu.VMEM((n_new, past + LANES), F32), pltpu.VMEM((rows_q, past), F32),
                        pltpu.SemaphoreType.DMA((2, 3))],
    )
    return pl.pallas_call(
        body,
        out_shape=jax.ShapeDtypeStruct((n_seq, rows_q, KV_W), F32),
        grid_spec=grid_spec,
        compiler_params=_cp(("arbitrary",),
                            blocks=[((2 * rows_q, KV_W), F32)],
                            scratch=[((2, 2 * KV_W + D_IDX, past), F32), ((n_new + rows_q, past + LANES), F32)],
                            values=[((KV_W, tl), F32), ((rows_q, tl), F32)] * 2),
        name="attn_sample",
    )(page_table, q2, iq2, iwc, k_new, v_new, ik_new, cache_k, cache_v, cache_ik)


def _rwkv_prep_body(p_ref, halo_ref, first_ref, mu_ref, w0_ref, wup_ref, a0_ref, aup_ref, kk_ref, ka_ref, rk_ref,
                    at_ref, rt_ref, kh_ref, bh_ref, kb_ref, bb_ref, v_ref, bonus_ref, pc_ref,
                    *, tb, chunk, blocks_per_seq):
    p = p_ref[...]
    row = _iota((tb, 1), 0)
    rolled = pltpu.roll(p, 1, 0)
    if blocks_per_seq is None:
        n_seq = tb // chunk
        first = first_ref[...]
        expanded = jnp.broadcast_to(first[:, None, :], (n_seq, chunk, N_SHIFT)).reshape(tb, N_SHIFT)
        prev = jnp.where(row % chunk == 0, expanded, rolled)
    else:
        i = pl.program_id(0)
        starts_seq = (i % blocks_per_seq) == 0
        row0 = jnp.where(starts_seq, first_ref[0], halo_ref[SUBLANES - 1:SUBLANES, :])
        prev = jnp.where(row == 0, row0, rolled)
    xs = p + (prev - p) * mu_ref[...]
    r = xs[:, 0:C_B]
    k = xs[:, C_B:2 * C_B]
    v = xs[:, 2 * C_B:3 * C_B]
    wa = xs[:, 3 * C_B:N_SHIFT]
    lane = _iota((tb, LANES), 1)
    t = jnp.where(lane < R_W, jnp.tanh(wa), wa).astype(BF16)
    log_w = -jax.nn.softplus(-(w0_ref[...] + _dot(t, wup_ref[...]))) - 0.5
    ld = -jnp.exp(log_w)
    a_sig = jax.nn.sigmoid(a0_ref[...] + _dot(t, aup_ref[...]))
    g = _head_ones()
    kk = k * kk_ref[...]
    k2 = k * (1.0 + (a_sig - 1.0) * ka_ref[...])
    rk = r * k2 * rk_ref[...]
    tri = jnp.where(_iota((chunk, chunk), 1) <= _iota((chunk, chunk), 0), 1.0, 0.0).astype(BF16)
    for ch in range(tb // chunk):
        rows = slice(ch * chunk, (ch + 1) * chunk)
        ld_c = ld[rows]
        cum = _dot_exact_lhs(tri, ld_c, 3)
        tot = cum[chunk - 1:chunk]
        rem = tot - cum
        pc_ref[0, ch:ch + 1, :] = jnp.exp(tot)
        e_cum = jnp.exp(cum)
        e_inv = jnp.exp(-cum)
        e_rem = jnp.exp(rem)
        e_prev = jnp.exp(cum - ld_c)
        for c in range(C_B // LANES):
            sl = slice(c * LANES, (c + 1) * LANES)
            kk_c = kk[rows, sl]
            nrm = jnp.maximum(jnp.sqrt(_head_sum(kk_c * kk_c, g)), 1e-12)
            kkn = kk_c / nrm
            bv = kkn * a_sig[rows, sl]
            at_ref[rows, sl] = (kkn * e_prev[:, sl]).astype(at_ref.dtype)
            rt_ref[rows, sl] = (r[rows, sl] * e_cum[:, sl]).astype(rt_ref.dtype)
            kh_ref[rows, sl] = (k2[rows, sl] * e_inv[:, sl]).astype(kh_ref.dtype)
            bh_ref[rows, sl] = (bv * e_inv[:, sl]).astype(bh_ref.dtype)
            kb_ref[rows, sl] = (k2[rows, sl] * e_rem[:, sl]).astype(kb_ref.dtype)
            bb_ref[rows, sl] = (bv * e_rem[:, sl]).astype(bb_ref.dtype)
            v_ref[rows, sl] = v[rows, sl].astype(v_ref.dtype)
            bonus_ref[rows, sl] = _head_sum(rk[rows, sl], g) * v[rows, sl]


def _rwkv_prep(p_rw, first_prev, params, tb, chunk, blocks_per_seq, store_dtype):
    rows = p_rw.shape[0]
    mu, w0, wup, a0, aup, kk, ka, rk = params
    nblk = rows // tb
    n_chunks = tb // chunk
    rowb = lambda i: (i, 0)
    fixed = lambda i: (0, 0)
    if blocks_per_seq is None:
        halo_spec = pl.BlockSpec((SUBLANES, N_SHIFT), fixed)
        first_spec = pl.BlockSpec((n_chunks, N_SHIFT), rowb)
    else:
        halo_spec = pl.BlockSpec((SUBLANES, N_SHIFT), lambda i: (jnp.maximum(i * (tb // SUBLANES) - 1, 0), 0))
        first_spec = pl.BlockSpec((1, 1, N_SHIFT), lambda i: (i // blocks_per_seq, 0, 0))
    vec = lambda n: pl.BlockSpec((1, n), fixed)
    wide = jax.ShapeDtypeStruct((rows, C_B), store_dtype)
    body = functools.partial(_rwkv_prep_body, tb=tb, chunk=chunk, blocks_per_seq=blocks_per_seq)
    return pl.pallas_call(
        body,
        out_shape=(wide,) * 7 + (jax.ShapeDtypeStruct((rows, C_B), F32),
                                 jax.ShapeDtypeStruct((nblk, n_chunks, C_B), F32)),
        grid=(nblk,),
        in_specs=[pl.BlockSpec((tb, N_SHIFT), rowb), halo_spec, first_spec, vec(N_SHIFT), vec(C_B),
                  pl.BlockSpec((LANES, C_B), fixed), vec(C_B), pl.BlockSpec((LANES, C_B), fixed),
                  vec(C_B), vec(C_B), vec(C_B)],
        out_specs=(pl.BlockSpec((tb, C_B), rowb),) * 8 + (pl.BlockSpec((1, n_chunks, C_B), lambda i: (i, 0, 0)),),
        compiler_params=_cp(("parallel",),
                            blocks=[((tb, N_SHIFT), F32), ((tb, 7 * C_B), store_dtype), ((tb, C_B), F32), ((2 * LANES, C_B), BF16)],
                            values=[((tb, N_SHIFT), F32)] * 4),
        name="rwkv_prep",
    )(p_rw, p_rw, first_prev, mu, w0, wup, a0, aup, kk, ka, rk)


def _rwkv_scan_body(at_ref, rt_ref, kh_ref, bh_ref, kb_ref, bb_ref, v_ref, bonus_ref, pc_ref, s0_ref, lnw_ref, lnb_ref,
                    y_ref, sout_ref, s_scr, *, chunk, n_chunks, n_par):
    c = pl.program_id(1)
    two = 2 * chunk
    n_pairs = N_HEADS_B // 2
    items = [(s, pr) for s in range(n_par) for pr in range(n_pairs)]

    @pl.when(c == 0)
    def _():
        s_scr[...] = jnp.zeros(s_scr.shape, F32)
        for i, (s, pr) in enumerate(items):
            s_scr[i, 0:HEAD_B, 0:HEAD_B] = s0_ref[s, 2 * pr]
            s_scr[i, HEAD_B:LANES, HEAD_B:LANES] = s0_ref[s, 2 * pr + 1]

    ri = _iota((two, two), 0)
    ci = _iota((two, two), 1)
    same = (ri // chunk) == (ci // chunk)
    strict = same & (ci < ri)
    incl = same & (ci <= ri)
    stack_mask = (_iota((two, LANES), 0) // chunk) == (_iota((two, LANES), 1) // HEAD_B)
    eye = jnp.where(ri == ci, 1.0, 0.0)
    g = _head_ones()
    n_factors = (chunk - 1).bit_length()

    def stack(ref, item):
        s, pr = item
        x = ref[s, :, pr * LANES:(pr + 1) * LANES].astype(BF16)
        return jnp.where(stack_mask, jnp.concatenate([x, x], axis=0), jnp.zeros((two, LANES), BF16))

    def mm(x, y):
        return _dot(x.astype(BF16), y.astype(BF16))

    pairs = range(len(items))
    sls = [slice(pr * LANES, (pr + 1) * LANES) for _, pr in items]
    a_s = [stack(at_ref, it) for it in items]
    b_s = [stack(bh_ref, it) for it in items]
    m_ab = [jnp.where(strict, _dot_nt(a_s[p], b_s[p]), 0.0) for p in pairs]
    pw = [-m for m in m_ab]
    t_inv = [eye + x for x in pw]
    for _ in range(n_factors - 1):
        pw = [mm(x, x) for x in pw]
        t_inv = [t + mm(t, x) for t, x in zip(t_inv, pw)]
    k_s = [stack(kh_ref, it) for it in items]
    r_s = [stack(rt_ref, it) for it in items]
    v_s = [stack(v_ref, it) for it in items]
    m_ak = [jnp.where(strict, _dot_nt(a_s[p], k_s[p]), 0.0).astype(BF16) for p in pairs]
    m_rb = [jnp.where(incl, _dot_nt(r_s[p], b_s[p]), 0.0).astype(BF16) for p in pairs]
    m_rk = [jnp.where(incl, _dot_nt(r_s[p], k_s[p]), 0.0).astype(BF16) for p in pairs]
    s_old = [s_scr[p] for p in pairs]
    s_bf = [s.astype(BF16) for s in s_old]
    w = [_dot_nt(a_s[p], s_bf[p]) + _dot(m_ak[p], v_s[p]) for p in pairs]
    u_bf = [(-mm(t_inv[p], w[p])).astype(BF16) for p in pairs]
    y2 = [_dot_nt(r_s[p], s_bf[p]) + _dot(m_rk[p], v_s[p]) + _dot(m_rb[p], u_bf[p]) for p in pairs]
    kb_s = [stack(kb_ref, it) for it in items]
    bb_s = [stack(bb_ref, it) for it in items]
    for p, (s, _) in enumerate(items):
        s_scr[p] = s_old[p] * pc_ref[s, 0, :, sls[p]] + _dot_tn(v_s[p], kb_s[p]) + _dot_tn(u_bf[p], bb_s[p])
    ys = [y[0:chunk] + y[chunk:two] for y in y2]
    means = [mm(y, g) * (1.0 / HEAD_B) for y in ys]
    devs = [y - m for y, m in zip(ys, means)]
    vars_ = [mm(d * d, g) * (1.0 / HEAD_B) for d in devs]
    for p, (s, _) in enumerate(items):
        sl = sls[p]
        y = devs[p] * lax.rsqrt(vars_[p] + GN_EPS) * lnw_ref[:, sl] + lnb_ref[:, sl] + bonus_ref[s, :, sl]
        y_ref[s, :, sl] = y.astype(y_ref.dtype)

    @pl.when(c == n_chunks - 1)
    def _():
        for i, (s, pr) in enumerate(items):
            sout_ref[s, 2 * pr] = s_scr[i, 0:HEAD_B, 0:HEAD_B]
            sout_ref[s, 2 * pr + 1] = s_scr[i, HEAD_B:LANES, HEAD_B:LANES]


def _rwkv_scan(prep, s0, lnw, lnb, n_seq, n_chunks, chunk, y_dtype):
    at, rt, kh, bh, kb, bb, vv, bonus, pc = prep
    n_par = min(SCAN_SEQS, n_seq)
    n_items = n_par * (N_HEADS_B // 2)
    shp = lambda a: a.reshape(n_seq, n_chunks * chunk, C_B)
    pc4 = pc.reshape(n_seq, n_chunks, 1, C_B)
    tok = pl.BlockSpec((n_par, chunk, C_B), lambda b, c: (b, c, 0))
    state = pl.BlockSpec((n_par, N_HEADS_B, HEAD_B, HEAD_B), lambda b, c: (b, 0, 0, 0))
    vec = pl.BlockSpec((1, C_B), lambda b, c: (0, 0))
    body = functools.partial(_rwkv_scan_body, chunk=chunk, n_chunks=n_chunks, n_par=n_par)
    return pl.pallas_call(
        body,
        out_shape=(jax.ShapeDtypeStruct((n_seq, n_chunks * chunk, C_B), y_dtype),
                   jax.ShapeDtypeStruct((n_seq, N_HEADS_B, HEAD_B, HEAD_B), F32)),
        grid=(n_seq // n_par, n_chunks),
        in_specs=[tok] * 8 + [pl.BlockSpec((n_par, 1, 1, C_B), lambda b, c: (b, c, 0, 0)), state, vec, vec],
        out_specs=(tok, state),
        scratch_shapes=[pltpu.VMEM((n_items, LANES, LANES), F32)],
        compiler_params=_cp(("parallel", "arbitrary"),
                            blocks=[((n_par * chunk, 9 * C_B), F32), ((n_par * 2 * N_HEADS_B, HEAD_B, LANES), F32)],
                            scratch=[((n_items, LANES, LANES), F32)],
                            values=[((n_items, 2 * chunk, LANES), F32)] * 24),
        name="rwkv_scan",
    )(shp(at), shp(rt), shp(kh), shp(bh), shp(kb), shp(bb), shp(vv), shp(bonus), pc4, s0, lnw, lnb)


def _merge_body(x_ref, attn_ref, rwo_ref, ga_ref, gb_ref, wo_ref, fw_ref, o_ref, *, final):
    up = lambda ref: ref[...].astype(F32)
    merged = up(ga_ref) * up(attn_ref) + up(gb_ref) * up(rwo_ref)
    xn = x_ref[...] + _dot(merged.astype(BF16), wo_ref[...])
    if final:
        xn = xn * lax.rsqrt(jnp.mean(xn * xn, axis=-1, keepdims=True) + RMS_EPS) * fw_ref[...]
    o_ref[...] = xn


def _merge(x2d, attn, rwo, gate_a, gate_b, wo, fw, tm, final):
    rows, d = x2d.shape
    rowb = lambda i: (i, 0)
    fixed = lambda i: (0, 0)
    return pl.pallas_call(
        functools.partial(_merge_body, final=final),
        out_shape=jax.ShapeDtypeStruct((rows, d), F32),
        grid=(rows // tm,),
        in_specs=[pl.BlockSpec((tm, d), rowb)] * 5 + [pl.BlockSpec((d, d), fixed), pl.BlockSpec((1, d), fixed)],
        out_specs=pl.BlockSpec((tm, d), rowb),
        compiler_params=_cp(("parallel",), blocks=[((tm, 6 * d), F32), ((d, d), BF16)], values=[((tm, d), F32)] * 2),
        name="merge_out",
    )(x2d, attn, rwo, gate_a, gate_b, wo, fw)


def _rope_tables(pos):
    inv = jnp.power(ROPE_THETA, -jnp.arange(HALF, dtype=F32) / HALF)
    ang = pos.astype(F32)[:, None] * inv[None, :]
    return jnp.cos(ang), jnp.sin(ang)


def _row_tables(cos, sin):
    return jnp.tile(cos, (1, LANES // HALF)), jnp.tile(jnp.concatenate([-sin, sin], axis=1), (1, LANES // HEAD_DIM))


def _pad_rows_to(x, rows, at):
    out = jnp.zeros((rows,) + x.shape[1:], x.dtype)
    return lax.dynamic_update_slice_in_dim(out, x, at, axis=0)


def _layer_weights(w_in_l, idx_k_ln_w_l, idx_k_ln_b_l, w_up_l, a_up_l):
    seg, off = {}, 0
    for name, size in _SEG_SIZES:
        seg[name] = w_in_l[:, off:off + size].astype(BF16)
        off += size
    pad = jnp.zeros((D_MODEL, LANES - D_IDX - N_IDX_HEADS), BF16)
    seg["ikw"] = jnp.concatenate([seg["ik"], seg["iw"], pad], axis=1)
    zeros = jnp.zeros((LANES - D_IDX,), F32)
    seg["lnw"] = jnp.concatenate([idx_k_ln_w_l, zeros]).reshape(1, LANES)
    seg["lnb"] = jnp.concatenate([idx_k_ln_b_l, zeros]).reshape(1, LANES)
    seg["wup"] = _pad_rows_to(w_up_l.astype(BF16), LANES, 0)
    seg["aup"] = _pad_rows_to(a_up_l.astype(BF16), LANES, R_W)
    return seg


def _rwkv_branch(p_rw, first_prev, s0, rw_params, lnw, lnb, n_seq, seq_len, chunk, tb, blocks_per_seq, store_dtype):
    prep = _rwkv_prep(p_rw, first_prev, rw_params, tb, chunk, blocks_per_seq, store_dtype)
    y, s_out = _rwkv_scan(prep, s0, lnw, lnb, n_seq, seq_len // chunk, chunk, store_dtype)
    return y.reshape(n_seq * seq_len, C_B), s_out


def kernel(x_prompt, x_sample, cache_k, cache_v, cache_idx_k, state_wkv, state_shift, page_table, norm_w, w_in, idx_k_ln_w, idx_k_ln_b, mu_shift, w0, w_up, a0, a_up, k_k, k_a, r_k, ln_x_w, ln_x_b, w_o, final_norm_w):
    batch, seq, _ = x_prompt.shape
    n_dec, n_new, _ = x_sample.shape
    depth = w_in.shape[0]
    n_pool, page = cache_k.shape[1], cache_k.shape[2]
    n_pages = page_table.shape[1]
    past = n_pages * page
    rows_p, rows_s = batch * seq, n_dec * n_new

    cos_p, sin_p = _rope_tables(jnp.arange(seq))
    cos_s, sin_s = _rope_tables(past + jnp.arange(n_new))
    cos_s, sin_s = jnp.tile(cos_s, (n_dec, 1)), jnp.tile(sin_s, (n_dec, 1))
    cos_pr, sin_pr = _row_tables(jnp.tile(cos_p, (batch, 1)), jnp.tile(sin_p, (batch, 1)))
    cos_sr, sin_sr = _row_tables(cos_s, sin_s)
    cos_pc, sin_pc = cos_p.T, sin_p.T

    xp = x_prompt.reshape(rows_p, D_MODEL)
    xs = x_sample.reshape(rows_s, D_MODEL)
    outs_p = {n: [] for n in ("k", "v", "ik", "wkv", "sh")}
    outs_s = {n: [] for n in ("k", "v", "ik", "wkv", "sh")}
    head_group = jnp.arange(N_HEADS_A) // HEADS_PER_KV
    kv_onehot = head_group[:, None] == jnp.arange(N_KV)[None, :]

    for l in range(depth):
        wl = _layer_weights(w_in[l], idx_k_ln_w[l], idx_k_ln_b[l], w_up[l], a_up[l])
        vec = lambda a: a.reshape(1, -1)
        rw_params = (vec(mu_shift[l]), vec(w0[l]), wl["wup"], vec(a0[l]), wl["aup"], vec(k_k[l]), vec(k_a[l]), vec(r_k[l]))
        lnw, lnb = vec(ln_x_w[l]), vec(ln_x_b[l])
        wo = w_o[l].astype(BF16)
        final = l == depth - 1
        fw = vec(final_norm_w)

        h = _rmsnorm(xp, norm_w[l], BF16, ROW_TILE)
        k_p, k_bf = _proj_rope(h, wl["k"], cos_pr, sin_pr, ROW_TILE, KV_W, "proj_k")
        v_p = _proj_plain(h, wl["v"], ROW_TILE, KV_W, "proj_v")
        ikw = _proj_ikw(h, wl["ikw"], cos_pr, sin_pr, wl["lnw"], wl["lnb"], ROW_TILE)
        ik_p = ikw[:, :D_IDX]
        qt, iqt, vt, iwt = _proj_t(h, wl["q"].T, wl["iq"].T, wl["v"].T, wl["iw"].T, cos_pc, sin_pc, batch, seq, ROW_TILE)
        gate_a = _proj_gate(h, wl["za"], wl["ga"], WIDE_ROW_TILE, GATE_COL_TILE, "proj_gate_a")
        gate_b = _proj_gate(h, wl["zb"], wl["gb"], WIDE_ROW_TILE, GATE_COL_TILE, "proj_gate_b")
        p_rw = _proj_plain(h, wl["rw"], WIDE_ROW_TILE, RW_COL_TILE, "proj_rw")
        attn = _attn_prompt(k_bf.reshape(batch, seq, KV_W), vt, ik_p.astype(BF16).reshape(batch, seq, D_IDX),
                            qt, iqt, iwt, min(TOPK_MAX, seq // 4), ATT_Q_TILE, ATT_KEY_CHUNK)
        rwo, wkv = _rwkv_branch(p_rw, jnp.zeros((batch, 1, N_SHIFT), F32),
                                jnp.zeros((batch, N_HEADS_B, HEAD_B, HEAD_B), F32), rw_params, lnw, lnb,
                                batch, seq, RWKV_CHUNK, RWKV_PREP_ROWS, seq // RWKV_PREP_ROWS, BF16)
        xp = _merge(xp, attn.reshape(rows_p, C_A), rwo, gate_a, gate_b, wo, fw, ROW_TILE, final)
        outs_p["k"].append(k_p.reshape(batch, seq, N_KV, HEAD_DIM))
        outs_p["v"].append(v_p.reshape(batch, seq, N_KV, HEAD_DIM))
        outs_p["ik"].append(ik_p.reshape(batch, seq, D_IDX))
        outs_p["wkv"].append(wkv)
        outs_p["sh"].append(p_rw.reshape(batch, seq, N_SHIFT)[:, -1])

        h = _rmsnorm(xs, norm_w[l], BF16, ROW_TILE)
        w_qkiq = jnp.concatenate([wl["q"], wl["k"], wl["iq"]], axis=1)
        qki, _ = _proj_rope(h, w_qkiq, cos_sr, sin_sr, ROW_TILE, KV_W, "proj_qkiq_s")
        q_s, k_s, iq_s = qki[:, :C_A], qki[:, C_A:C_A + KV_W], qki[:, C_A + KV_W:]
        v_s = _proj_plain(h, wl["v"], ROW_TILE, KV_W, "proj_v_s")
        ikw = _proj_ikw(h, wl["ikw"], cos_sr, sin_sr, wl["lnw"], wl["lnb"], ROW_TILE)
        ik_s, iw_s = ikw[:, :D_IDX], ikw[:, D_IDX:D_IDX + N_IDX_HEADS]
        gate_a = _proj_gate(h, wl["za"], wl["ga"], ROW_TILE, GATE_COL_TILE, "proj_gate_a_s")
        gate_b = _proj_gate(h, wl["zb"], wl["gb"], ROW_TILE, GATE_COL_TILE, "proj_gate_b_s")
        p_rw = _proj_plain(h, wl["rw"], ROW_TILE, RW_COL_TILE, "proj_rw_s")
        q4 = q_s.reshape(n_dec, n_new, N_HEADS_A, HEAD_DIM).transpose(0, 2, 1, 3)
        q2 = jnp.where(kv_onehot[None, :, None, :, None], q4[:, :, :, None, :], 0.0)
        q2 = q2.reshape(n_dec, N_HEADS_A * n_new, KV_W).astype(BF16)
        iq2 = iq_s.reshape(n_dec, n_new, N_IDX_HEADS, D_IDX).transpose(0, 2, 1, 3)
        iq2 = iq2.reshape(n_dec, N_IDX_HEADS * n_new, D_IDX).astype(BF16)
        iwc = iw_s.reshape(n_dec, n_new, N_IDX_HEADS).transpose(0, 2, 1).reshape(n_dec, N_IDX_HEADS * n_new, 1)
        ck_t = jnp.transpose(cache_k[l], (0, 2, 3, 1)).reshape(n_pool, KV_W, page)
        cv_t = jnp.transpose(cache_v[l], (0, 2, 3, 1)).reshape(n_pool, KV_W, page)
        cik_t = jnp.transpose(cache_idx_k[l], (0, 2, 1))
        o2 = _attn_sample(page_table, q2, iq2, iwc, k_s.reshape(n_dec, n_new, KV_W), v_s.reshape(n_dec, n_new, KV_W),
                          ik_s.reshape(n_dec, n_new, D_IDX), ck_t, cv_t, cik_t, min(TOPK_MAX, (past + n_new) // 4),
                          SAMPLE_KEY_CHUNK)
        o5 = o2.reshape(n_dec, N_HEADS_A, n_new, N_KV, HEAD_DIM)
        attn_s = o5[:, jnp.arange(N_HEADS_A), :, head_group, :]
        attn_s = attn_s.transpose(1, 2, 0, 3).reshape(rows_s, C_A)
        rwo, wkv = _rwkv_branch(p_rw, state_shift[l], state_wkv[l], rw_params, lnw, lnb,
                                n_dec, n_new, n_new, SUBLANES * n_new, None, F32)
        xs = _merge(xs, attn_s, rwo, gate_a, gate_b, wo, fw, ROW_TILE, final)
        outs_s["k"].append(k_s.reshape(n_dec, n_new, N_KV, HEAD_DIM))
        outs_s["v"].append(v_s.reshape(n_dec, n_new, N_KV, HEAD_DIM))
        outs_s["ik"].append(ik_s.reshape(n_dec, n_new, D_IDX))
        outs_s["wkv"].append(wkv)
        outs_s["sh"].append(p_rw.reshape(n_dec, n_new, N_SHIFT)[:, -1])

    st = lambda d, n: jnp.stack(d[n])
    return (xp.reshape(batch, seq, D_MODEL), xs.reshape(n_dec, n_new, D_MODEL),
            st(outs_p, "k"), st(outs_p, "v"), st(outs_p, "ik"), st(outs_p, "wkv"), st(outs_p, "sh"),
            st(outs_s, "k"), st(outs_s, "v"), st(outs_s, "ik"), st(outs_s, "wkv"), st(outs_s, "sh"))
```
